```python
import jax, jax.numpy as jnp
from jax import lax
import numpy as np

D_MODEL = 1024
BATCH = 8
SEQ = 2048
DEPTH = 2
DEC_BATCH = 128
DEC_SEQ = 8
PAST_LEN = 16384
PAGE_SIZE = 128

N_MEM = 256
D_A = 512
K_A = 31
D_B = 1024
K_B = 4
H_B = 8
BW_B = D_B // H_B
LRU_C = 8.0
D_C = 512
H_C = 4
DH_C = D_C // H_C
N_BRANCH = 3
D_IN = 2 * D_A + 2 * D_B + D_C + N_BRANCH * D_MODEL
SPLITS = (2 * D_A, 2 * D_A + D_B, 2 * D_A + 2 * D_B, 2 * D_A + 2 * D_B + D_C)
D_FF = 2816
N_EXPERTS = 8
TOP_K = 2
D_FF_E = 2816
N_DENSE = (DEPTH + 1) // 2
N_MOE = DEPTH // 2
ALPHA = (2.0 * DEPTH) ** 0.25
BETA = (8.0 * DEPTH) ** -0.25
LN_EPS = 1e-5

kernel_name = 'hybrid_conv_rglru_memxattn_decoder_step'


def layer_norm(x, g, b):
    xf = x.astype(jnp.float32)
    mu = jnp.mean(xf, axis=-1, keepdims=True)
    var = jnp.mean(jnp.square(xf - mu), axis=-1, keepdims=True)
    y = (xf - mu) * lax.rsqrt(var + LN_EPS) * g.astype(jnp.float32) + b.astype(jnp.float32)
    return y.astype(x.dtype)


def causal_depthwise_conv(x_full, w, b):
    c = x_full.shape[-1]
    y = lax.conv_general_dilated(x_full, w[:, None, :].astype(x_full.dtype), window_strides=(1,),
                                 padding='VALID', dimension_numbers=('NWC', 'WIO', 'NWC'),
                                 feature_group_count=c)
    return y + b


def rg_lru(x, h0, w_a, b_a, w_x, b_x, lam):
    bsz, t, _ = x.shape
    xh = x.reshape(bsz, t, H_B, BW_B)
    r = jax.nn.sigmoid(jnp.einsum('bthi,hij->bthj', xh, w_a).reshape(bsz, t, D_B) + b_a)
    i = jax.nn.sigmoid(jnp.einsum('bthi,hij->bthj', xh, w_x).reshape(bsz, t, D_B) + b_x)
    log_a = -LRU_C * r.astype(jnp.float32) * jax.nn.softplus(-lam.astype(jnp.float32))
    a = jnp.exp(log_a)
    u = jnp.sqrt(-jnp.expm1(2.0 * log_a)) * (i * x).astype(jnp.float32)
    u = u.at[:, 0].add(a[:, 0] * h0.astype(jnp.float32))

    def combine(left, right):
        return (left[0] * right[0], right[0] * left[1] + right[1])

    _, h = lax.associative_scan(combine, (a, u), axis=1)
    return h.astype(x.dtype), h[:, -1].astype(x.dtype)


def token_mixing(x, buf_a, buf_b, h0, mem_k, mem_v,
                 w_in, b_in, w_conv_a, b_conv_a, ln_a_g, ln_a_b, w_a_out,
                 w_conv_b, b_conv_b, w_rg_a, b_rg_a, w_rg_x, b_rg_x, lru_lambda, w_b_out,
                 w_c_out, w_mix_out):
    bsz, t, _ = x.shape
    z = x @ w_in + b_in
    z_a, z_b, z_g, q, z_m = jnp.split(z, SPLITS, axis=-1)
    u = z_a[..., :D_A] * jax.nn.sigmoid(z_a[..., D_A:])
    u_full = jnp.concatenate([buf_a, u], axis=1)
    c_a = causal_depthwise_conv(u_full, w_conv_a, b_conv_a)
    y_a = jax.nn.silu(layer_norm(c_a, ln_a_g, ln_a_b)) @ w_a_out
    xb_full = jnp.concatenate([buf_b, z_b], axis=1)
    c_b = causal_depthwise_conv(xb_full, w_conv_b, b_conv_b)
    h_seq, h_last = rg_lru(c_b, h0, w_rg_a, b_rg_a, w_rg_x, b_rg_x, lru_lambda)
    y_b = (h_seq * jax.nn.gelu(z_g)) @ w_b_out
    qh = q.reshape(bsz, t, H_C, DH_C)
    s = jnp.einsum('bthd,bmhd->bhtm', qh, mem_k, preferred_element_type=jnp.float32) * (DH_C ** -0.5)
    p = jax.nn.softmax(s, axis=-1)
    o = jnp.einsum('bhtm,bmhd->bthd', p.astype(x.dtype), mem_v).reshape(bsz, t, D_C)
    y_c = o @ w_c_out
    g = jax.nn.sigmoid(z_m).reshape(bsz, t, N_BRANCH, D_MODEL)
    merged = g[:, :, 0] * y_a + g[:, :, 1] * y_b + g[:, :, 2] * y_c
    return (merged @ w_mix_out, u_full[:, -(K_A - 1):], xb_full[:, -(K_B - 1):], h_last)


def swiglu(x, w_g, w_u, w_d):
    return (jax.nn.silu(x @ w_g) * (x @ w_u)) @ w_d


def moe_swiglu(x, w_router, b_router, w_gate, w_up, w_down):
    logits = (x @ w_router + b_router).astype(jnp.float32)
    top_v, top_i = lax.top_k(logits, TOP_K)
    probs = jax.nn.softmax(top_v, axis=-1)
    gates = jnp.sum(jax.nn.one_hot(top_i, N_EXPERTS, dtype=jnp.float32) * probs[..., None], axis=-2).astype(x.dtype)
    out = jnp.zeros_like(x)
    for e in range(N_EXPERTS):
        out = out + gates[..., e:e + 1] * swiglu(x, w_gate[e], w_up[e], w_down[e])
    return out


def setup_inputs(seed: int = 0) -> dict:
    key = jax.random.key(seed)
    ks = jax.random.split(key, 40)

    def nrm(k, shape, scale):
        return jax.random.normal(k, shape, jnp.float32) * scale

    u = jax.random.uniform(ks[20], (DEPTH, D_B), jnp.float32, minval=0.9, maxval=0.999)
    s = u ** (1.0 / LRU_C)
    lru_lambda = jnp.log(s) - jnp.log1p(-s)
    return {
        'x_prompt': nrm(ks[0], (BATCH, SEQ, D_MODEL), 1.0),
        'x_sample': nrm(ks[1], (DEC_BATCH, DEC_SEQ, D_MODEL), 1.0),
        'state_conv_a': nrm(ks[2], (DEPTH, DEC_BATCH, K_A - 1, D_A), 0.5),
        'state_conv_b': nrm(ks[3], (DEPTH, DEC_BATCH, K_B - 1, D_B), 1.0),
        'state_rglru': nrm(ks[4], (DEPTH, DEC_BATCH, D_B), 0.5),
        'cache_mem_k': nrm(ks[5], (DEPTH, DEC_BATCH, N_MEM, H_C, DH_C), 1.0),
        'cache_mem_v': nrm(ks[6], (DEPTH, DEC_BATCH, N_MEM, H_C, DH_C), 1.0),
        'mem_prompt': nrm(ks[7], (BATCH, N_MEM, D_MODEL), 1.0),
        'w_in': nrm(ks[8], (DEPTH, D_MODEL, D_IN), D_MODEL ** -0.5),
        'b_in': nrm(ks[9], (DEPTH, D_IN), 0.02),
        'w_conv_a': nrm(ks[10], (DEPTH, K_A, D_A), K_A ** -0.5),
        'b_conv_a': nrm(ks[11], (DEPTH, D_A), 0.02),
        'ln_a_g': 1.0 + nrm(ks[12], (DEPTH, D_A), 0.02),
        'ln_a_b': nrm(ks[13], (DEPTH, D_A), 0.02),
        'w_a_out': nrm(ks[14], (DEPTH, D_A, D_MODEL), D_A ** -0.5),
        'w_conv_b': nrm(ks[15], (DEPTH, K_B, D_B), K_B ** -0.5),
        'b_conv_b': nrm(ks[16], (DEPTH, D_B), 0.02),
        'w_rg_a': nrm(ks[17], (DEPTH, H_B, BW_B, BW_B), BW_B ** -0.5),
        'b_rg_a': nrm(ks[18], (DEPTH, D_B), 0.02),
        'w_rg_x': nrm(ks[19], (DEPTH, H_B, BW_B, BW_B), BW_B ** -0.5),
        'b_rg_x': nrm(ks[21], (DEPTH, D_B), 0.02),
        'lru_lambda': lru_lambda,
        'w_b_out': nrm(ks[22], (DEPTH, D_B, D_MODEL), D_B ** -0.5),
        'w_mem_kv': nrm(ks[23], (DEPTH, D_MODEL, 2 * D_C), D_MODEL ** -0.5),
        'w_c_out': nrm(ks[24], (DEPTH, D_C, D_MODEL), D_C ** -0.5),
        'w_mix_out': nrm(ks[25], (DEPTH, D_MODEL, D_MODEL), BETA * D_MODEL ** -0.5),
        'ln1_g': 1.0 + nrm(ks[26], (DEPTH, D_MODEL), 0.02),
        'ln1_b': nrm(ks[27], (DEPTH, D_MODEL), 0.02),
        'w_ff_gate': nrm(ks[28], (N_DENSE, D_MODEL, D_FF), D_MODEL ** -0.5),
        'w_ff_up': nrm(ks[29], (N_DENSE, D_MODEL, D_FF), D_MODEL ** -0.5),
        'w_ff_down': nrm(ks[30], (N_DENSE, D_FF, D_MODEL), BETA * D_FF ** -0.5),
        'w_router': nrm(ks[31], (N_MOE, D_MODEL, N_EXPERTS), D_MODEL ** -0.5),
        'b_router': nrm(ks[32], (N_MOE, N_EXPERTS), 0.01),
        'w_e_gate': nrm(ks[33], (N_MOE, N_EXPERTS, D_MODEL, D_FF_E), D_MODEL ** -0.5),
        'w_e_up': nrm(ks[34], (N_MOE, N_EXPERTS, D_MODEL, D_FF_E), D_MODEL ** -0.5),
        'w_e_down': nrm(ks[35], (N_MOE, N_EXPERTS, D_FF_E, D_MODEL), BETA * D_FF_E ** -0.5),
        'ln2_g': 1.0 + nrm(ks[36], (DEPTH, D_MODEL), 0.02),
        'ln2_b': nrm(ks[37], (DEPTH, D_MODEL), 0.02),
    }


def reference(x_prompt, x_sample, state_conv_a, state_conv_b, state_rglru, cache_mem_k, cache_mem_v,
              mem_prompt, w_in, b_in, w_conv_a, b_conv_a, ln_a_g, ln_a_b, w_a_out,
              w_conv_b, b_conv_b, w_rg_a, b_rg_a, w_rg_x, b_rg_x, lru_lambda, w_b_out,
              w_mem_kv, w_c_out, w_mix_out, ln1_g, ln1_b, w_ff_gate, w_ff_up, w_ff_down,
              w_router, b_router, w_e_gate, w_e_up, w_e_down, ln2_g, ln2_b):

    def run_layer(l, x, buf_a, buf_b, h0, mem_k, mem_v):
        mix, nb_a, nb_b, h_last = token_mixing(
            x, buf_a, buf_b, h0, mem_k, mem_v,
            w_in[l], b_in[l], w_conv_a[l], b_conv_a[l], ln_a_g[l], ln_a_b[l], w_a_out[l],
            w_conv_b[l], b_conv_b[l], w_rg_a[l], b_rg_a[l], w_rg_x[l], b_rg_x[l], lru_lambda[l], w_b_out[l],
            w_c_out[l], w_mix_out[l])
        x = layer_norm(ALPHA * x + mix, ln1_g[l], ln1_b[l])
        j = l // 2
        if l % 2 == 0:
            f = swiglu(x, w_ff_gate[j], w_ff_up[j], w_ff_down[j])
        else:
            f = moe_swiglu(x, w_router[j], b_router[j], w_e_gate[j], w_e_up[j], w_e_down[j])
        x = layer_norm(ALPHA * x + f, ln2_g[l], ln2_b[l])
        return x, nb_a, nb_b, h_last

    b_p = x_prompt.shape[0]
    n_mem = mem_prompt.shape[1]
    y_p = x_prompt
    y_s = x_sample
    p_conv_a, p_conv_b, p_h, p_mk, p_mv = [], [], [], [], []
    s_conv_a, s_conv_b, s_h = [], [], []
    for l in range(DEPTH):
        kv = mem_prompt @ w_mem_kv[l]
        mk = kv[..., :D_C].reshape(b_p, n_mem, H_C, DH_C)
        mv = kv[..., D_C:].reshape(b_p, n_mem, H_C, DH_C)
        zero_a = jnp.zeros((b_p, K_A - 1, D_A), x_prompt.dtype)
        zero_b = jnp.zeros((b_p, K_B - 1, D_B), x_prompt.dtype)
        zero_h = jnp.zeros((b_p, D_B), x_prompt.dtype)
        y_p, na, nb, nh = run_layer(l, y_p, zero_a, zero_b, zero_h, mk, mv)
        p_conv_a.append(na)
        p_conv_b.append(nb)
        p_h.append(nh)
        p_mk.append(mk)
        p_mv.append(mv)
        y_s, na, nb, nh = run_layer(l, y_s, state_conv_a[l], state_conv_b[l], state_rglru[l],
                                    cache_mem_k[l], cache_mem_v[l])
        s_conv_a.append(na)
        s_conv_b.append(nb)
        s_h.append(nh)

    return (y_p, y_s, jnp.stack(p_conv_a), jnp.stack(p_conv_b), jnp.stack(p_h), jnp.stack(p_mk), jnp.stack(p_mv),
            jnp.stack(s_conv_a), jnp.stack(s_conv_b), jnp.stack(s_h))
```

```python
import functools

import jax
import jax.numpy as jnp
from jax import lax
from jax.experimental import pallas as pl
from jax.experimental.pallas import tpu as pltpu

D_MODEL = 1024
N_MEM = 256
D_A = 512
K_A = 31
D_B = 1024
K_B = 4
H_B = 8
BW_B = D_B // H_B
LRU_C = 8.0
D_C = 512
H_C = 4
DH_C = D_C // H_C
N_BRANCH = 3
D_IN = 2 * D_A + 2 * D_B + D_C + N_BRANCH * D_MODEL
D_FF = 2816
N_EXPERTS = 8
DEPTH = 2
ALPHA = (2.0 * DEPTH) ** 0.25
LN_EPS = 1e-5

Z_BLOCK = 512
Q_BLOCK = (2 * D_A + 2 * D_B) // Z_BLOCK
ZM_BLOCK = Q_BLOCK + 1

LANES = 128
VMEM_LIMIT = 56 * 1024 * 1024

BF16 = jnp.bfloat16
F32 = jnp.float32


def _params(*sem):
    return pltpu.CompilerParams(dimension_semantics=sem, vmem_limit_bytes=VMEM_LIMIT)


def _layer_norm(x, g, b):
    mu = jnp.mean(x, axis=-1, keepdims=True)
    xc = x - mu
    var = jnp.mean(xc * xc, axis=-1, keepdims=True)
    return xc * lax.rsqrt(var + LN_EPS) * g + b


def _bdot(a, b):
    return jnp.dot(a.astype(BF16), b.astype(BF16), preferred_element_type=F32)


def _matmul_bias_kernel(x_ref, w_ref, b_ref, o_ref, xb_ref):
    @pl.when(pl.program_id(1) == 0)
    def _():
        xb_ref[...] = x_ref[...].astype(BF16)

    o_ref[...] = jnp.dot(xb_ref[...], w_ref[...].astype(BF16),
                         preferred_element_type=F32) + b_ref[...]


def matmul_bias(x, w, b, tm, tn):
    m, k = x.shape
    n = w.shape[1]
    return pl.pallas_call(
        _matmul_bias_kernel,
        grid=(m // tm, n // tn),
        in_specs=[pl.BlockSpec((tm, k), lambda i, j: (i, 0)),
                  pl.BlockSpec((k, tn), lambda i, j: (0, j)),
                  pl.BlockSpec((1, tn), lambda i, j: (0, j))],
        out_specs=pl.BlockSpec((tm, tn), lambda i, j: (i, j)),
        out_shape=jax.ShapeDtypeStruct((m, n), F32),
        scratch_shapes=[pltpu.VMEM((tm, k), BF16)],
        compiler_params=_params("arbitrary", "arbitrary"),
    )(x, w, b.reshape(1, n))


def _attention_kernel(q_ref, k_ref, v_ref, o_ref, *, sb):
    for s in range(sb):
        for h in range(H_C):
            cols = slice(h * DH_C, (h + 1) * DH_C)
            qh = q_ref[s, :, cols].astype(BF16)
            kh = k_ref[s, :, cols].astype(BF16)
            vh = v_ref[s, :, cols].astype(BF16)
            sc = lax.dot_general(qh, kh, (((1,), (1,)), ((), ())),
                                 preferred_element_type=F32) * (DH_C ** -0.5)
            e = jnp.exp(sc - jnp.max(sc, axis=-1, keepdims=True))
            p = e / jnp.sum(e, axis=-1, keepdims=True)
            o_ref[s, :, cols] = jnp.dot(p.astype(BF16), vh, preferred_element_type=F32)


def attention(q, k, v, sb, tq):
    s, t, _ = q.shape
    return pl.pallas_call(
        functools.partial(_attention_kernel, sb=sb),
        grid=(s // sb, t // tq),
        in_specs=[pl.BlockSpec((sb, tq, D_C), lambda i, j: (i, j, 0)),
                  pl.BlockSpec((sb, N_MEM, D_C), lambda i, j: (i, 0, 0)),
                  pl.BlockSpec((sb, N_MEM, D_C), lambda i, j: (i, 0, 0))],
        out_specs=pl.BlockSpec((sb, tq, D_C), lambda i, j: (i, j, 0)),
        out_shape=jax.ShapeDtypeStruct((s, t, D_C), F32),
        compiler_params=_params("arbitrary", "arbitrary"),
    )(q, k, v)


def _mix_kernel(za_ref, zb_ref, zg_ref, m0a_ref, m0b_ref, m1a_ref, m1b_ref, m2a_ref, m2b_ref,
                o_ref, x_ref, sa_ref, sb_ref, h0_ref,
                wca_ref, bca_ref, lag_ref, lab_ref, wao_ref,
                wcb_ref, bcb_ref, wrg_ref, bra_ref, brx_ref, lam_ref, wbo_ref,
                wco_ref, wmo_ref, l1g_ref, l1b_ref,
                x1_ref, nsa_ref, nsb_ref, hl_ref,
                ua_buf, xb_buf, a_buf, u_buf, hs_buf, h_carry, *, tt, nb, nt):
    r = tt * nb
    ha = (K_A - 1) * nb
    hb = (K_B - 1) * nb
    i = pl.program_id(1)

    @pl.when(i == 0)
    def _():
        ua_buf[0:ha, :] = sa_ref[...].reshape(ha, D_A)
        xb_buf[0:hb, :] = sb_ref[...].reshape(hb, D_B)
        h_carry[...] = h0_ref[...]

    za = za_ref[...].reshape(r, 2 * D_A)
    ua_buf[ha:ha + r, :] = za[:, :D_A] * jax.nn.sigmoid(za[:, D_A:])
    c_a = jnp.broadcast_to(bca_ref[...], (r, D_A))
    for k in range(K_A):
        c_a = c_a + wca_ref[k:k + 1, :] * ua_buf[k * nb:k * nb + r, :]
    y_a = _bdot(jax.nn.silu(_layer_norm(c_a, lag_ref[...], lab_ref[...])), wao_ref[...])
    nsa_ref[...] = ua_buf[r:r + ha, :].reshape(K_A - 1, nb, D_A)

    xb_buf[hb:hb + r, :] = zb_ref[...].reshape(r, D_B)
    c_b = jnp.broadcast_to(bcb_ref[...], (r, D_B))
    for k in range(K_B):
        c_b = c_b + wcb_ref[k:k + 1, :] * xb_buf[k * nb:k * nb + r, :]
    nsb_ref[...] = xb_buf[r:r + hb, :].reshape(K_B - 1, nb, D_B)
    c_bf = c_b.astype(BF16)
    ra, rx = [], []
    for h in range(H_B):
        gh = jnp.dot(c_bf[:, h * BW_B:(h + 1) * BW_B], wrg_ref[h], preferred_element_type=F32)
        ra.append(gh[:, :BW_B])
        rx.append(gh[:, BW_B:])
    gate_r = jax.nn.sigmoid(jnp.concatenate(ra, axis=1) + bra_ref[...])
    gate_i = jax.nn.sigmoid(jnp.concatenate(rx, axis=1) + brx_ref[...])
    log_a = (-LRU_C) * gate_r * jax.nn.softplus(-lam_ref[...])
    a_buf[...] = jnp.exp(log_a)
    th = jnp.tanh(log_a)
    u_buf[...] = jnp.sqrt(-2.0 * th / (1.0 - th)) * (gate_i * c_b)
    h = h_carry[...]
    for t in range(tt):
        rows = slice(t * nb, (t + 1) * nb)
        h = a_buf[rows, :] * h + u_buf[rows, :]
        hs_buf[rows, :] = h
    h_carry[...] = h
    hl_ref[...] = h
    y_b = _bdot(hs_buf[...] * jax.nn.gelu(zg_ref[...].reshape(r, D_B)), wbo_ref[...])

    if nt > 1:
        ua_buf[0:ha, :] = ua_buf[r:r + ha, :]
        xb_buf[0:hb, :] = xb_buf[r:r + hb, :]

    y_c = _bdot(o_ref[...].reshape(r, D_C), wco_ref[...])

    def gate(ref):
        return jax.nn.sigmoid(ref[...].reshape(r, Z_BLOCK))

    half = D_MODEL // 2
    lo = (gate(m0a_ref) * y_a[:, :half] + gate(m1a_ref) * y_b[:, :half]
          + gate(m2a_ref) * y_c[:, :half])
    hi = (gate(m0b_ref) * y_a[:, half:] + gate(m1b_ref) * y_b[:, half:]
          + gate(m2b_ref) * y_c[:, half:])
    mix = _bdot(jnp.concatenate([lo, hi], axis=1), wmo_ref[...])
    x = x_ref[...].reshape(r, D_MODEL)
    x1_ref[...] = _layer_norm(ALPHA * x + mix, l1g_ref[...], l1b_ref[...]).reshape(tt, nb, D_MODEL)


def mix(z3, o_tm, x_tm, sa_tm, sb_tm, h0, w, tt, nb):
    t, nbt, _ = x_tm.shape
    nt = t // tt
    assert nt == 1 or tt >= K_A - 1
    r = tt * nb
    grid = (nbt // nb, nt)

    def zspec(width, blk):
        return pl.BlockSpec((tt, nb, width), lambda g, i: (i, g, blk))

    def full(a):
        nd = a.ndim
        return pl.BlockSpec(a.shape, lambda g, i: (0,) * nd)

    weights = [w["w_conv_a"], w["b_conv_a"], w["ln_a_g"], w["ln_a_b"], w["w_a_out"],
               w["w_conv_b"], w["b_conv_b"], w["w_rg"], w["b_rg_a"], w["b_rg_x"], w["lam"],
               w["w_b_out"], w["w_c_out"], w["w_mix_out"], w["ln1_g"], w["ln1_b"]]
    in_specs = ([zspec(2 * D_A, 0), zspec(D_B, 1), zspec(D_B, 2)]
                + [zspec(Z_BLOCK, ZM_BLOCK + j) for j in range(6)]
                + [zspec(D_C, 0), zspec(D_MODEL, 0),
                   pl.BlockSpec((K_A - 1, nb, D_A), lambda g, i: (0, g, 0)),
                   pl.BlockSpec((K_B - 1, nb, D_B), lambda g, i: (0, g, 0)),
                   pl.BlockSpec((nb, D_B), lambda g, i: (g, 0))]
                + [full(a) for a in weights])
    out_specs = [zspec(D_MODEL, 0),
                 pl.BlockSpec((K_A - 1, nb, D_A), lambda g, i: (0, g, 0)),
                 pl.BlockSpec((K_B - 1, nb, D_B), lambda g, i: (0, g, 0)),
                 pl.BlockSpec((nb, D_B), lambda g, i: (g, 0))]
    out_shape = [jax.ShapeDtypeStruct((t, nbt, D_MODEL), F32),
                 jax.ShapeDtypeStruct((K_A - 1, nbt, D_A), F32),
                 jax.ShapeDtypeStruct((K_B - 1, nbt, D_B), F32),
                 jax.ShapeDtypeStruct((nbt, D_B), F32)]
    scratch = [pltpu.VMEM(((K_A - 1) * nb + r, D_A), F32),
               pltpu.VMEM(((K_B - 1) * nb + r, D_B), F32),
               pltpu.VMEM((r, D_B), F32), pltpu.VMEM((r, D_B), F32), pltpu.VMEM((r, D_B), F32),
               pltpu.VMEM((nb, D_B), F32)]
    return pl.pallas_call(
        functools.partial(_mix_kernel, tt=tt, nb=nb, nt=nt),
        grid=grid, in_specs=in_specs, out_specs=out_specs, out_shape=out_shape,
        scratch_shapes=scratch,
        compiler_params=_params("arbitrary", "arbitrary"),
    )(*([z3] * 9), o_tm, x_tm, sa_tm, sb_tm, h0, *weights)


def _ffn_kernel(x_ref, wg_ref, wu_ref, wd_ref, g_ref, b_ref, o_ref, xb_ref, acc_ref):
    j = pl.program_id(1)

    @pl.when(j == 0)
    def _():
        xb_ref[...] = x_ref[...].astype(BF16)
        acc_ref[...] = jnp.zeros_like(acc_ref)

    xb = xb_ref[...]
    hg = jnp.dot(xb, wg_ref[...].astype(BF16), preferred_element_type=F32)
    hu = jnp.dot(xb, wu_ref[...].astype(BF16), preferred_element_type=F32)
    acc_ref[...] += _bdot(jax.nn.silu(hg) * hu, wd_ref[...])

    @pl.when(j == pl.num_programs(1) - 1)
    def _():
        o_ref[...] = _layer_norm(ALPHA * x_ref[...] + acc_ref[...], g_ref[...], b_ref[...])


def ffn_dense(x, wg, wu, wd, g, b, tm, tf):
    m = x.shape[0]
    return pl.pallas_call(
        _ffn_kernel,
        grid=(m // tm, D_FF // tf),
        in_specs=[pl.BlockSpec((tm, D_MODEL), lambda i, j: (i, 0)),
                  pl.BlockSpec((D_MODEL, tf), lambda i, j: (0, j)),
                  pl.BlockSpec((D_MODEL, tf), lambda i, j: (0, j)),
                  pl.BlockSpec((tf, D_MODEL), lambda i, j: (j, 0)),
                  pl.BlockSpec((1, D_MODEL), lambda i, j: (0, 0)),
                  pl.BlockSpec((1, D_MODEL), lambda i, j: (0, 0))],
        out_specs=pl.BlockSpec((tm, D_MODEL), lambda i, j: (i, 0)),
        out_shape=jax.ShapeDtypeStruct((m, D_MODEL), F32),
        scratch_shapes=[pltpu.VMEM((tm, D_MODEL), BF16), pltpu.VMEM((tm, D_MODEL), F32)],
        compiler_params=_params("arbitrary", "arbitrary"),
    )(x, wg, wu, wd, g.reshape(1, -1), b.reshape(1, -1))


def _router_kernel(x_ref, w_ref, b_ref, gates_ref):
    logits = jnp.dot(x_ref[...], w_ref[...], precision=lax.Precision.HIGHEST,
                     preferred_element_type=F32) + b_ref[...]
    lane = lax.broadcasted_iota(jnp.int32, logits.shape, 1)
    neg = jnp.float32(-jnp.inf)
    l1 = jnp.where(lane < N_EXPERTS, logits, neg)
    m1 = jnp.max(l1, axis=1, keepdims=True)
    i1 = jnp.min(jnp.where(l1 == m1, lane, LANES), axis=1, keepdims=True)
    l2 = jnp.where(lane == i1, neg, l1)
    m2 = jnp.max(l2, axis=1, keepdims=True)
    i2 = jnp.min(jnp.where(l2 == m2, lane, LANES), axis=1, keepdims=True)
    e2 = jnp.exp(m2 - m1)
    den = 1.0 + e2
    gates_ref[...] = jnp.where(lane == i1, 1.0 / den, 0.0) + jnp.where(lane == i2, e2 / den, 0.0)


def router(x, w_router, b_router, tm):
    m = x.shape[0]
    wp = jnp.zeros((D_MODEL, LANES), F32).at[:, :N_EXPERTS].set(w_router)
    bp = jnp.zeros((1, LANES), F32).at[0, :N_EXPERTS].set(b_router)
    return pl.pallas_call(
        _router_kernel,
        grid=(m // tm,),
        in_specs=[pl.BlockSpec((tm, D_MODEL), lambda i: (i, 0)),
                  pl.BlockSpec((D_MODEL, LANES), lambda i: (0, 0)),
                  pl.BlockSpec((1, LANES), lambda i: (0, 0))],
        out_specs=pl.BlockSpec((tm, LANES), lambda i: (i, 0)),
        out_shape=jax.ShapeDtypeStruct((m, LANES), F32),
        compiler_params=_params("arbitrary"),
    )(x, wp, bp)


def _moe_kernel(x_ref, gates_ref, wg_ref, wu_ref, wd_ref, g_ref, b_ref, o_ref, xb_ref, acc_ref):
    e = pl.program_id(1)
    j = pl.program_id(2)

    @pl.when((e == 0) & (j == 0))
    def _():
        xb_ref[...] = x_ref[...].astype(BF16)
        acc_ref[...] = jnp.zeros_like(acc_ref)

    gates = gates_ref[...]
    lane = lax.broadcasted_iota(jnp.int32, gates.shape, 1)
    ge = jnp.sum(jnp.where(lane == e, gates, 0.0), axis=1, keepdims=True)
    xb = xb_ref[...]
    hg = jnp.dot(xb, wg_ref[...].astype(BF16), preferred_element_type=F32)
    hu = jnp.dot(xb, wu_ref[...].astype(BF16), preferred_element_type=F32)
    acc_ref[...] += ge * _bdot(jax.nn.silu(hg) * hu, wd_ref[...])

    @pl.when((e == pl.num_programs(1) - 1) & (j == pl.num_programs(2) - 1))
    def _():
        o_ref[...] = _layer_norm(ALPHA * x_ref[...] + acc_ref[...], g_ref[...], b_ref[...])


def moe_dense(x, gates, wg, wu, wd, g, b, tm, tf):
    m = x.shape[0]
    return pl.pallas_call(
        _moe_kernel,
        grid=(m // tm, N_EXPERTS, D_FF // tf),
        in_specs=[pl.BlockSpec((tm, D_MODEL), lambda i, e, j: (i, 0)),
                  pl.BlockSpec((tm, LANES), lambda i, e, j: (i, 0)),
                  pl.BlockSpec((None, D_MODEL, tf), lambda i, e, j: (e, 0, j)),
                  pl.BlockSpec((None, D_MODEL, tf), lambda i, e, j: (e, 0, j)),
                  pl.BlockSpec((None, tf, D_MODEL), lambda i, e, j: (e, j, 0)),
                  pl.BlockSpec((1, D_MODEL), lambda i, e, j: (0, 0)),
                  pl.BlockSpec((1, D_MODEL), lambda i, e, j: (0, 0))],
        out_specs=pl.BlockSpec((tm, D_MODEL), lambda i, e, j: (i, 0)),
        out_shape=jax.ShapeDtypeStruct((m, D_MODEL), F32),
        scratch_shapes=[pltpu.VMEM((tm, D_MODEL), BF16), pltpu.VMEM((tm, D_MODEL), F32)],
        compiler_params=_params("arbitrary", "arbitrary", "arbitrary"),
    )(x, gates, wg, wu, wd, g.reshape(1, -1), b.reshape(1, -1))


def _layer_weights(l, p):
    row = lambda a: a[l].reshape(1, -1)
    return {
        "w_conv_a": p["w_conv_a"][l], "b_conv_a": row(p["b_conv_a"]),
        "ln_a_g": row(p["ln_a_g"]), "ln_a_b": row(p["ln_a_b"]),
        "w_a_out": p["w_a_out"][l].astype(BF16),
        "w_conv_b": p["w_conv_b"][l], "b_conv_b": row(p["b_conv_b"]),
        "w_rg": jnp.concatenate([p["w_rg_a"][l], p["w_rg_x"][l]], axis=-1).astype(BF16),
        "b_rg_a": row(p["b_rg_a"]), "b_rg_x": row(p["b_rg_x"]), "lam": row(p["lru_lambda"]),
        "w_b_out": p["w_b_out"][l].astype(BF16), "w_c_out": p["w_c_out"][l].astype(BF16),
        "w_mix_out": p["w_mix_out"][l].astype(BF16),
        "ln1_g": row(p["ln1_g"]), "ln1_b": row(p["ln1_b"]),
    }


def _run_layer(l, p, w, x_tm, sa_tm, sb_tm, h0, mem_k, mem_v, tt, nb, att_sb, att_tq):
    t, nbt, _ = x_tm.shape
    m = t * nbt
    z = matmul_bias(x_tm.reshape(m, D_MODEL), p["w_in"][l], p["b_in"][l], tm=1024, tn=Z_BLOCK)
    z3 = z.reshape(t, nbt, D_IN)
    q = jnp.transpose(z3[:, :, Q_BLOCK * Z_BLOCK:(Q_BLOCK + 1) * Z_BLOCK], (1, 0, 2))
    o = attention(q, mem_k, mem_v, att_sb, att_tq)
    o_tm = jnp.transpose(o, (1, 0, 2))
    x1, nsa, nsb, hl = mix(z3, o_tm, x_tm, sa_tm, sb_tm, h0, w, tt, nb)
    x1 = x1.reshape(m, D_MODEL)
    j = l // 2
    if l % 2 == 0:
        x2 = ffn_dense(x1, p["w_ff_gate"][j], p["w_ff_up"][j], p["w_ff_down"][j],
                       p["ln2_g"][l], p["ln2_b"][l], tm=1024, tf=256)
    else:
        gates = router(x1, p["w_router"][j], p["b_router"][j], tm=1024)
        x2 = moe_dense(x1, gates, p["w_e_gate"][j], p["w_e_up"][j], p["w_e_down"][j],
                       p["ln2_g"][l], p["ln2_b"][l], tm=1024, tf=256)
    return x2.reshape(t, nbt, D_MODEL), nsa, nsb, hl


def _to_tm(a):
    return jnp.transpose(a, (1, 0, 2))


def kernel(x_prompt, x_sample, state_conv_a, state_conv_b, state_rglru, cache_mem_k, cache_mem_v, mem_prompt, w_in, b_in, w_conv_a, b_conv_a, ln_a_g, ln_a_b, w_a_out, w_conv_b, b_conv_b, w_rg_a, b_rg_a, w_rg_x, b_rg_x, lru_lambda, w_b_out, w_mem_kv, w_c_out, w_mix_out, ln1_g, ln1_b, w_ff_gate, w_ff_up, w_ff_down, w_router, b_router, w_e_gate, w_e_up, w_e_down, ln2_g, ln2_b):
    p = dict(w_in=w_in, b_in=b_in, w_conv_a=w_conv_a, b_conv_a=b_conv_a, ln_a_g=ln_a_g,
             ln_a_b=ln_a_b, w_a_out=w_a_out, w_conv_b=w_conv_b, b_conv_b=b_conv_b,
             w_rg_a=w_rg_a, b_rg_a=b_rg_a, w_rg_x=w_rg_x, b_rg_x=b_rg_x, lru_lambda=lru_lambda,
             w_b_out=w_b_out, w_c_out=w_c_out, w_mix_out=w_mix_out, ln1_g=ln1_g, ln1_b=ln1_b,
             w_ff_gate=w_ff_gate, w_ff_up=w_ff_up, w_ff_down=w_ff_down, w_router=w_router,
             b_router=b_router, w_e_gate=w_e_gate, w_e_up=w_e_up, w_e_down=w_e_down,
             ln2_g=ln2_g, ln2_b=ln2_b)
    bp, tp, _ = x_prompt.shape
    bs, ts, _ = x_sample.shape
    n_mem = mem_prompt.shape[1]

    yp = _to_tm(x_prompt)
    ys = _to_tm(x_sample)
    zero_a = jnp.zeros((K_A - 1, bp, D_A), F32)
    zero_b = jnp.zeros((K_B - 1, bp, D_B), F32)
    zero_h = jnp.zeros((bp, D_B), F32)
    zero_bias = jnp.zeros((2 * D_C,), F32)
    outs = {k: [] for k in ("pa", "pb", "ph", "pk", "pv", "sa", "sb", "sh")}
    for l in range(DEPTH):
        w = _layer_weights(l, p)
        kv = matmul_bias(mem_prompt.reshape(bp * n_mem, D_MODEL), w_mem_kv[l], zero_bias,
                         tm=1024, tn=512).reshape(bp, n_mem, 2 * D_C)
        mk = kv[..., :D_C]
        mv = kv[..., D_C:]
        yp, na, nb_, nh = _run_layer(l, p, w, yp, zero_a, zero_b, zero_h, mk, mv,
                                     tt=32, nb=8, att_sb=1, att_tq=512)
        outs["pa"].append(_to_tm(na))
        outs["pb"].append(_to_tm(nb_))
        outs["ph"].append(nh)
        outs["pk"].append(mk.reshape(bp, n_mem, H_C, DH_C))
        outs["pv"].append(mv.reshape(bp, n_mem, H_C, DH_C))
        ys, na, nb_, nh = _run_layer(l, p, w, ys, _to_tm(state_conv_a[l]), _to_tm(state_conv_b[l]),
                                     state_rglru[l],
                                     cache_mem_k[l].reshape(bs, n_mem, D_C),
                                     cache_mem_v[l].reshape(bs, n_mem, D_C),
                                     tt=ts, nb=32, att_sb=8, att_tq=ts)
        outs["sa"].append(_to_tm(na))
        outs["sb"].append(_to_tm(nb_))
        outs["sh"].append(nh)

    st = lambda k: jnp.stack(outs[k])
    return (_to_tm(yp), _to_tm(ys), st("pa"), st("pb"), st("ph"), st("pk"), st("pv"),
            st("sa"), st("sb"), st("sh"))
```

```python
import functools

import jax
import jax.numpy as jnp
from jax import lax
from jax.experimental import pallas as pl
from jax.experimental.pallas import tpu as pltpu

D_MODEL = 1024
N_MEM = 256
D_A = 512
K_A = 31
D_B = 1024
K_B = 4
H_B = 8
BW_B = D_B // H_B
LRU_C = 8.0
D_C = 512
H_C = 4
DH_C = D_C // H_C
N_BRANCH = 3
D_IN = 2 * D_A + 2 * D_B + D_C + N_BRANCH * D_MODEL
D_FF = 2816
N_EXPERTS = 8
TOP_K = 2
DEPTH = 2
ALPHA = (2.0 * DEPTH) ** 0.25
LN_EPS = 1e-5

Z_BLOCK = 512
Q_BLOCK = (2 * D_A + 2 * D_B) // Z_BLOCK
ZM_BLOCK = Q_BLOCK + 1

LANES = 128
VMEM_LIMIT = 56 * 1024 * 1024

TM = 1024
TF = 256
ROW_CHUNK = 2048

BF16 = jnp.bfloat16
F32 = jnp.float32


def _params(*sem):
    return pltpu.CompilerParams(dimension_semantics=sem, vmem_limit_bytes=VMEM_LIMIT)


def _layer_norm(x, g, b):
    mu = jnp.mean(x, axis=-1, keepdims=True)
    xc = x - mu
    var = jnp.mean(xc * xc, axis=-1, keepdims=True)
    return xc * lax.rsqrt(var + LN_EPS) * g + b


def _bdot(a, b):
    return jnp.dot(a.astype(BF16), b.astype(BF16), preferred_element_type=F32)


def _matmul_bias_kernel(x_ref, w_ref, b_ref, o_ref, xb_ref):
    @pl.when(pl.program_id(1) == 0)
    def _():
        xb_ref[...] = x_ref[...].astype(BF16)

    o_ref[...] = jnp.dot(xb_ref[...], w_ref[...].astype(BF16),
                         preferred_element_type=F32) + b_ref[...]


def matmul_bias(x, w, b, l, tn, name):
    m, k = x.shape
    n = w.shape[2]
    return pl.pallas_call(
        _matmul_bias_kernel,
        grid=(m // TM, n // tn),
        in_specs=[pl.BlockSpec((TM, k), lambda i, j: (i, 0)),
                  pl.BlockSpec((None, k, tn), lambda i, j: (l, 0, j)),
                  pl.BlockSpec((None, 1, tn), lambda i, j: (l, 0, j))],
        out_specs=pl.BlockSpec((TM, tn), lambda i, j: (i, j)),
        out_shape=jax.ShapeDtypeStruct((m, n), F32),
        scratch_shapes=[pltpu.VMEM((TM, k), BF16)],
        compiler_params=_params("arbitrary", "arbitrary"),
        name=name,
    )(x, w, b)


def _attention_kernel(q_ref, k_ref, v_ref, o_ref, *, sb):
    for s in range(sb):
        for h in range(H_C):
            cols = slice(h * DH_C, (h + 1) * DH_C)
            qh = q_ref[s, :, cols].astype(BF16)
            kh = k_ref[s, :, cols].astype(BF16)
            vh = v_ref[s, :, cols].astype(BF16)
            sc = lax.dot_general(qh, kh, (((1,), (1,)), ((), ())),
                                 preferred_element_type=F32) * (DH_C ** -0.5)
            e = jnp.exp(sc - jnp.max(sc, axis=-1, keepdims=True))
            p = e / jnp.sum(e, axis=-1, keepdims=True)
            o_ref[s, :, cols] = jnp.dot(p.astype(BF16), vh, preferred_element_type=F32)


def attention(q, k, v, sb, tq):
    s, t, _ = q.shape
    return pl.pallas_call(
        functools.partial(_attention_kernel, sb=sb),
        grid=(s // sb, t // tq),
        in_specs=[pl.BlockSpec((sb, tq, D_C), lambda i, j: (i, j, 0)),
                  pl.BlockSpec((sb, N_MEM, D_C), lambda i, j: (i, 0, 0)),
                  pl.BlockSpec((sb, N_MEM, D_C), lambda i, j: (i, 0, 0))],
        out_specs=pl.BlockSpec((sb, tq, D_C), lambda i, j: (i, j, 0)),
        out_shape=jax.ShapeDtypeStruct((s, t, D_C), F32),
        compiler_params=_params("arbitrary", "arbitrary"),
        name="attention",
    )(q, k, v)


def _mix_kernel(za_ref, zb_ref, zg_ref, m0a_ref, m0b_ref, m1a_ref, m1b_ref, m2a_ref, m2b_ref,
                o_ref, x_ref, sa_ref, sb_ref, h0_ref,
                wca_ref, bca_ref, lag_ref, lab_ref, wao_ref,
                wcb_ref, bcb_ref, wrg_ref, bra_ref, brx_ref, lam_ref, wbo_ref,
                wco_ref, wmo_ref, l1g_ref, l1b_ref,
                x1_ref, nsa_ref, nsb_ref, hl_ref,
                ua_buf, xb_buf, a_buf, u_buf, hs_buf, h_carry, *, tt, nb, nt):
    r = tt * nb
    ha = (K_A - 1) * nb
    hb = (K_B - 1) * nb
    i = pl.program_id(1)

    @pl.when(i == 0)
    def _():
        ua_buf[0:ha, :] = sa_ref[...].reshape(ha, D_A)
        xb_buf[0:hb, :] = sb_ref[...].reshape(hb, D_B)
        h_carry[...] = h0_ref[...]

    za = za_ref[...].reshape(r, 2 * D_A)
    ua_buf[ha:ha + r, :] = za[:, :D_A] * jax.nn.sigmoid(za[:, D_A:])
    c_a = jnp.broadcast_to(bca_ref[...], (r, D_A))
    for k in range(K_A):
        c_a = c_a + wca_ref[k:k + 1, :] * ua_buf[k * nb:k * nb + r, :]
    y_a = _bdot(jax.nn.silu(_layer_norm(c_a, lag_ref[...], lab_ref[...])), wao_ref[...])
    nsa_ref[...] = ua_buf[r:r + ha, :].reshape(K_A - 1, nb, D_A)

    xb_buf[hb:hb + r, :] = zb_ref[...].reshape(r, D_B)
    c_b = jnp.broadcast_to(bcb_ref[...], (r, D_B))
    for k in range(K_B):
        c_b = c_b + wcb_ref[k:k + 1, :] * xb_buf[k * nb:k * nb + r, :]
    nsb_ref[...] = xb_buf[r:r + hb, :].reshape(K_B - 1, nb, D_B)
    c_bf = c_b.astype(BF16)
    ra, rx = [], []
    for h in range(H_B):
        gh = jnp.dot(c_bf[:, h * BW_B:(h + 1) * BW_B], wrg_ref[h], preferred_element_type=F32)
        ra.append(gh[:, :BW_B])
        rx.append(gh[:, BW_B:])
    gate_r = jax.nn.sigmoid(jnp.concatenate(ra, axis=1) + bra_ref[...])
    gate_i = jax.nn.sigmoid(jnp.concatenate(rx, axis=1) + brx_ref[...])
    log_a = (-LRU_C) * gate_r * jax.nn.softplus(-lam_ref[...])
    a_buf[...] = jnp.exp(log_a)
    th = jnp.tanh(log_a)
    u_buf[...] = jnp.sqrt(-2.0 * th / (1.0 - th)) * (gate_i * c_b)
    h = h_carry[...]
    for t in range(tt):
        rows = slice(t * nb, (t + 1) * nb)
        h = a_buf[rows, :] * h + u_buf[rows, :]
        hs_buf[rows, :] = h
    h_carry[...] = h
    hl_ref[...] = h
    y_b = _bdot(hs_buf[...] * jax.nn.gelu(zg_ref[...].reshape(r, D_B)), wbo_ref[...])

    if nt > 1:
        ua_buf[0:ha, :] = ua_buf[r:r + ha, :]
        xb_buf[0:hb, :] = xb_buf[r:r + hb, :]

    y_c = _bdot(o_ref[...].reshape(r, D_C), wco_ref[...])

    def gate(ref):
        return jax.nn.sigmoid(ref[...].reshape(r, Z_BLOCK))

    half = D_MODEL // 2
    lo = (gate(m0a_ref) * y_a[:, :half] + gate(m1a_ref) * y_b[:, :half]
          + gate(m2a_ref) * y_c[:, :half])
    hi = (gate(m0b_ref) * y_a[:, half:] + gate(m1b_ref) * y_b[:, half:]
          + gate(m2b_ref) * y_c[:, half:])
    mix = _bdot(jnp.concatenate([lo, hi], axis=1), wmo_ref[...])
    x = x_ref[...].reshape(r, D_MODEL)
    x1_ref[...] = _layer_norm(ALPHA * x + mix, l1g_ref[...], l1b_ref[...]).reshape(tt, nb, D_MODEL)


def mix(z3, o_tm, x_tm, sa_tm, sb_tm, h0, w, tt, nb):
    t, nbt, _ = x_tm.shape
    nt = t // tt
    assert nt == 1 or tt >= K_A - 1
    r = tt * nb
    grid = (nbt // nb, nt)

    def zspec(width, blk):
        return pl.BlockSpec((tt, nb, width), lambda g, i: (i, g, blk))

    def full(a):
        nd = a.ndim
        return pl.BlockSpec(a.shape, lambda g, i: (0,) * nd)

    weights = [w["w_conv_a"], w["b_conv_a"], w["ln_a_g"], w["ln_a_b"], w["w_a_out"],
               w["w_conv_b"], w["b_conv_b"], w["w_rg"], w["b_rg_a"], w["b_rg_x"], w["lam"],
               w["w_b_out"], w["w_c_out"], w["w_mix_out"], w["ln1_g"], w["ln1_b"]]
    in_specs = ([zspec(2 * D_A, 0), zspec(D_B, 1), zspec(D_B, 2)]
                + [zspec(Z_BLOCK, ZM_BLOCK + j) for j in range(6)]
                + [zspec(D_C, 0), zspec(D_MODEL, 0),
                   pl.BlockSpec((K_A - 1, nb, D_A), lambda g, i: (0, g, 0)),
                   pl.BlockSpec((K_B - 1, nb, D_B), lambda g, i: (0, g, 0)),
                   pl.BlockSpec((nb, D_B), lambda g, i: (g, 0))]
                + [full(a) for a in weights])
    out_specs = [zspec(D_MODEL, 0),
                 pl.BlockSpec((K_A - 1, nb, D_A), lambda g, i: (0, g, 0)),
                 pl.BlockSpec((K_B - 1, nb, D_B), lambda g, i: (0, g, 0)),
                 pl.BlockSpec((nb, D_B), lambda g, i: (g, 0))]
    out_shape = [jax.ShapeDtypeStruct((t, nbt, D_MODEL), F32),
                 jax.ShapeDtypeStruct((K_A - 1, nbt, D_A), F32),
                 jax.ShapeDtypeStruct((K_B - 1, nbt, D_B), F32),
                 jax.ShapeDtypeStruct((nbt, D_B), F32)]
    scratch = [pltpu.VMEM(((K_A - 1) * nb + r, D_A), F32),
               pltpu.VMEM(((K_B - 1) * nb + r, D_B), F32),
               pltpu.VMEM((r, D_B), F32), pltpu.VMEM((r, D_B), F32), pltpu.VMEM((r, D_B), F32),
               pltpu.VMEM((nb, D_B), F32)]
    return pl.pallas_call(
        functools.partial(_mix_kernel, tt=tt, nb=nb, nt=nt),
        grid=grid, in_specs=in_specs, out_specs=out_specs, out_shape=out_shape,
        scratch_shapes=scratch,
        compiler_params=_params("arbitrary", "arbitrary"),
        name="mix",
    )(*([z3] * 9), o_tm, x_tm, sa_tm, sb_tm, h0, *weights)


def _ffn_kernel(x_ref, wg_ref, wu_ref, wd_ref, g_ref, b_ref, o_ref, xb_ref, acc_ref):
    j = pl.program_id(1)

    @pl.when(j == 0)
    def _():
        xb_ref[...] = x_ref[...].astype(BF16)
        acc_ref[...] = jnp.zeros_like(acc_ref)

    xb = xb_ref[...]
    hg = jnp.dot(xb, wg_ref[...].astype(BF16), preferred_element_type=F32)
    hu = jnp.dot(xb, wu_ref[...].astype(BF16), preferred_element_type=F32)
    acc_ref[...] += _bdot(jax.nn.silu(hg) * hu, wd_ref[...])

    @pl.when(j == pl.num_programs(1) - 1)
    def _():
        o_ref[...] = _layer_norm(ALPHA * x_ref[...] + acc_ref[...], g_ref[...], b_ref[...])


def ffn_dense(x, wg, wu, wd, g, b, l):
    m = x.shape[0]
    return pl.pallas_call(
        _ffn_kernel,
        grid=(m // TM, D_FF // TF),
        in_specs=[pl.BlockSpec((TM, D_MODEL), lambda i, j: (i, 0)),
                  pl.BlockSpec((None, D_MODEL, TF), lambda i, j: (l, 0, j)),
                  pl.BlockSpec((None, D_MODEL, TF), lambda i, j: (l, 0, j)),
                  pl.BlockSpec((None, TF, D_MODEL), lambda i, j: (l, j, 0)),
                  pl.BlockSpec((1, D_MODEL), lambda i, j: (0, 0)),
                  pl.BlockSpec((1, D_MODEL), lambda i, j: (0, 0))],
        out_specs=pl.BlockSpec((TM, D_MODEL), lambda i, j: (i, 0)),
        out_shape=jax.ShapeDtypeStruct((m, D_MODEL), F32),
        scratch_shapes=[pltpu.VMEM((TM, D_MODEL), BF16), pltpu.VMEM((TM, D_MODEL), F32)],
        compiler_params=_params("arbitrary", "arbitrary"),
        name="ffn_dense",
    )(x, wg, wu, wd, g.reshape(1, -1), b.reshape(1, -1))


def _router_kernel(x_ref, w_ref, b_ref, sel_ref, prob_ref):
    logits = jnp.dot(x_ref[...], w_ref[...], precision=lax.Precision.HIGHEST,
                     preferred_element_type=F32) + b_ref[...]
    lane = lax.broadcasted_iota(jnp.int32, logits.shape, 1)
    neg = jnp.float32(-jnp.inf)
    l1 = jnp.where(lane < N_EXPERTS, logits, neg)
    m1 = jnp.max(l1, axis=1, keepdims=True)
    i1 = jnp.min(jnp.where(l1 == m1, lane, LANES), axis=1, keepdims=True)
    l2 = jnp.where(lane == i1, neg, l1)
    m2 = jnp.max(l2, axis=1, keepdims=True)
    i2 = jnp.min(jnp.where(l2 == m2, lane, LANES), axis=1, keepdims=True)
    e2 = jnp.exp(m2 - m1)
    den = 1.0 + e2
    sel_ref[...] = jnp.where(lane == 0, i1, jnp.where(lane == 1, i2, 0))
    prob_ref[...] = jnp.where(lane == 0, 1.0 / den, jnp.where(lane == 1, e2 / den, 0.0))


def router(x, w_router, b_router):
    m = x.shape[0]
    wp = jnp.zeros((D_MODEL, LANES), F32).at[:, :N_EXPERTS].set(w_router)
    bp = jnp.zeros((1, LANES), F32).at[0, :N_EXPERTS].set(b_router)
    return pl.pallas_call(
        _router_kernel,
        grid=(m // TM,),
        in_specs=[pl.BlockSpec((TM, D_MODEL), lambda i: (i, 0)),
                  pl.BlockSpec((D_MODEL, LANES), lambda i: (0, 0)),
                  pl.BlockSpec((1, LANES), lambda i: (0, 0))],
        out_specs=[pl.BlockSpec((TM, LANES), lambda i: (i, 0)),
                   pl.BlockSpec((TM, LANES), lambda i: (i, 0))],
        out_shape=[jax.ShapeDtypeStruct((m, LANES), jnp.int32),
                   jax.ShapeDtypeStruct((m, LANES), F32)],
        compiler_params=_params("arbitrary"),
        name="router",
    )(x, wp, bp)


def _route(sel, tm, n_tiles):
    e = sel[:, :TOP_K].reshape(-1)
    onehot = (e[:, None] == jnp.arange(N_EXPERTS, dtype=jnp.int32)[None, :]).astype(jnp.int32)
    csum = jnp.cumsum(onehot, axis=0)
    rank = jnp.sum(onehot * csum, axis=1) - 1
    counts = csum[-1]
    padded = ((counts + tm - 1) // tm) * tm
    ends = jnp.cumsum(padded)
    starts = ends - padded
    dest = jnp.sum(onehot * starts[None, :], axis=1) + rank
    first_row = jnp.arange(n_tiles, dtype=jnp.int32) * tm
    tile_expert = jnp.sum((first_row[:, None] >= ends[None, :]).astype(jnp.int32), axis=1)
    tile_expert = jnp.minimum(tile_expert, N_EXPERTS - 1)
    n_used = (ends[-1] // tm).reshape(1)
    return dest.astype(jnp.int32), tile_expert.astype(jnp.int32), n_used.astype(jnp.int32)


def _row_copy(src_ref, dst_ref, sem, s, d, n):
    return pltpu.make_async_copy(src_ref.at[pl.ds(s, n), :], dst_ref.at[pl.ds(d, n), :], sem)


def _row_permute_kernel(sidx_ref, didx_ref, src_ref, *rest):
    dst_ref, sem = rest[-2:]

    def issue(i, carry):
        _row_copy(src_ref, dst_ref, sem, sidx_ref[0, i], didx_ref[0, i], 1).start()
        return carry

    lax.fori_loop(0, ROW_CHUNK, issue, 0)
    _row_copy(src_ref, dst_ref, sem, 0, 0, ROW_CHUNK).wait()


def row_permute(src, sidx, didx, dst_rows, init, name):
    n = sidx.shape[0]
    nc = n // ROW_CHUNK
    idx_spec = pl.BlockSpec((None, 1, ROW_CHUNK), lambda c: (c, 0, 0), memory_space=pltpu.SMEM)
    any_spec = pl.BlockSpec(memory_space=pl.ANY)
    extra = () if init is None else (init,)
    return pl.pallas_call(
        _row_permute_kernel,
        grid=(nc,),
        in_specs=[idx_spec, idx_spec, any_spec] + [any_spec] * len(extra),
        out_specs=any_spec,
        out_shape=jax.ShapeDtypeStruct((dst_rows, src.shape[1]), src.dtype),
        scratch_shapes=[pltpu.SemaphoreType.DMA(())],
        input_output_aliases={} if init is None else {3: 0},
        compiler_params=_params("arbitrary"),
        name=name,
    )(sidx.reshape(nc, 1, ROW_CHUNK), didx.reshape(nc, 1, ROW_CHUNK), src, *extra)


def _gmm_kernel(te_ref, nu_ref, x_ref, wg_ref, wu_ref, wd_ref, y_ref, xb_ref):
    del te_ref
    i = pl.program_id(0)
    j = pl.program_id(1)

    @pl.when((i >= nu_ref[0]) & (j == 0))
    def _():
        y_ref[...] = jnp.zeros_like(y_ref)

    @pl.when(i < nu_ref[0])
    def _():
        @pl.when(j == 0)
        def _():
            xb_ref[...] = x_ref[...].astype(BF16)

        xb = xb_ref[...]
        hg = jnp.dot(xb, wg_ref[...].astype(BF16), preferred_element_type=F32)
        hu = jnp.dot(xb, wu_ref[...].astype(BF16), preferred_element_type=F32)
        y = _bdot(jax.nn.silu(hg) * hu, wd_ref[...])

        @pl.when(j == 0)
        def _():
            y_ref[...] = y

        @pl.when(j > 0)
        def _():
            y_ref[...] += y


def gmm(xs, tile_expert, n_used, wg, wu, wd, l):
    rows = xs.shape[0]
    n_tiles = rows // TM
    nf = D_FF // TF

    def tile(i, nu):
        return jnp.minimum(i, nu[0] - 1)

    def ff(i, j, nu):
        return jnp.where(i < nu[0], j, nf - 1)

    grid_spec = pltpu.PrefetchScalarGridSpec(
        num_scalar_prefetch=2,
        grid=(n_tiles, nf),
        in_specs=[pl.BlockSpec((TM, D_MODEL), lambda i, j, te, nu: (i, 0)),
                  pl.BlockSpec((None, None, D_MODEL, TF),
                               lambda i, j, te, nu: (l, te[tile(i, nu)], 0, ff(i, j, nu))),
                  pl.BlockSpec((None, None, D_MODEL, TF),
                               lambda i, j, te, nu: (l, te[tile(i, nu)], 0, ff(i, j, nu))),
                  pl.BlockSpec((None, None, TF, D_MODEL),
                               lambda i, j, te, nu: (l, te[tile(i, nu)], ff(i, j, nu), 0))],
        out_specs=pl.BlockSpec((TM, D_MODEL), lambda i, j, te, nu: (i, 0)),
        scratch_shapes=[pltpu.VMEM((TM, D_MODEL), BF16)],
    )
    return pl.pallas_call(
        _gmm_kernel,
        grid_spec=grid_spec,
        out_shape=jax.ShapeDtypeStruct((rows, D_MODEL), F32),
        compiler_params=_params("arbitrary", "arbitrary"),
        name="gmm",
    )(tile_expert, n_used, xs, wg, wu, wd)


def _combine_kernel(x_ref, y0_ref, y1_ref, prob_ref, g_ref, b_ref, o_ref):
    f = prob_ref[:, 0:1] * y0_ref[...] + prob_ref[:, 1:2] * y1_ref[...]
    o_ref[...] = _layer_norm(ALPHA * x_ref[...] + f, g_ref[...], b_ref[...])


def combine(x, ypair, prob, g, b):
    m = x.shape[0]
    return pl.pallas_call(
        _combine_kernel,
        grid=(m // TM,),
        in_specs=[pl.BlockSpec((TM, D_MODEL), lambda i: (i, 0)),
                  pl.BlockSpec((None, TM, D_MODEL), lambda i: (0, i, 0)),
                  pl.BlockSpec((None, TM, D_MODEL), lambda i: (1, i, 0)),
                  pl.BlockSpec((TM, LANES), lambda i: (i, 0)),
                  pl.BlockSpec((1, D_MODEL), lambda i: (0, 0)),
                  pl.BlockSpec((1, D_MODEL), lambda i: (0, 0))],
        out_specs=pl.BlockSpec((TM, D_MODEL), lambda i: (i, 0)),
        out_shape=jax.ShapeDtypeStruct((m, D_MODEL), F32),
        compiler_params=_params("arbitrary"),
        name="combine",
    )(x, ypair, ypair, prob, g.reshape(1, -1), b.reshape(1, -1))


def moe_routed(xs_list, p, l):
    j = l // 2
    routed = [router(x, p["w_router"][j], p["b_router"][j]) for x in xs_list]
    sel = jnp.concatenate([r[0] for r in routed], axis=0)
    n_pairs = TOP_K * sel.shape[0]
    n_tiles = n_pairs // TM + N_EXPERTS
    dest, tile_expert, n_used = _route(sel, TM, n_tiles)

    grouped = jnp.zeros((n_tiles * TM, D_MODEL), F32)
    offs = 0
    for x in xs_list:
        np_ = TOP_K * x.shape[0]
        tok = jnp.arange(np_, dtype=jnp.int32) // TOP_K
        grouped = row_permute(x, tok, dest[offs:offs + np_], n_tiles * TM, grouped, "dispatch")
        offs += np_
    y = gmm(grouped, tile_expert, n_used, p["w_e_gate"], p["w_e_up"], p["w_e_down"], j)

    outs = []
    offs = 0
    for x, (_, prob) in zip(xs_list, routed):
        m = x.shape[0]
        np_ = TOP_K * m
        pair = jnp.arange(np_, dtype=jnp.int32)
        slot = (pair % TOP_K) * m + pair // TOP_K
        ypair = row_permute(y, dest[offs:offs + np_], slot, np_, None, "collect")
        outs.append(combine(x, ypair.reshape(TOP_K, m, D_MODEL), prob,
                            p["ln2_g"][l], p["ln2_b"][l]))
        offs += np_
    return outs


def _layer_weights(l, p):
    row = lambda a: a[l].reshape(1, -1)
    return {
        "w_conv_a": p["w_conv_a"][l], "b_conv_a": row(p["b_conv_a"]),
        "ln_a_g": row(p["ln_a_g"]), "ln_a_b": row(p["ln_a_b"]),
        "w_a_out": p["w_a_out"][l].astype(BF16),
        "w_conv_b": p["w_conv_b"][l], "b_conv_b": row(p["b_conv_b"]),
        "w_rg": jnp.concatenate([p["w_rg_a"][l], p["w_rg_x"][l]], axis=-1).astype(BF16),
        "b_rg_a": row(p["b_rg_a"]), "b_rg_x": row(p["b_rg_x"]), "lam": row(p["lru_lambda"]),
        "w_b_out": p["w_b_out"][l].astype(BF16), "w_c_out": p["w_c_out"][l].astype(BF16),
        "w_mix_out": p["w_mix_out"][l].astype(BF16),
        "ln1_g": row(p["ln1_g"]), "ln1_b": row(p["ln1_b"]),
    }


def _mix_layer(l, p, w, x_tm, sa_tm, sb_tm, h0, mem_k, mem_v, tt, nb, att_sb, att_tq):
    t, nbt, _ = x_tm.shape
    m = t * nbt
    z = matmul_bias(x_tm.reshape(m, D_MODEL), p["w_in"], p["b_in"].reshape(DEPTH, 1, D_IN), l,
                    Z_BLOCK, "in_proj")
    z3 = z.reshape(t, nbt, D_IN)
    q = jnp.transpose(z3[:, :, Q_BLOCK * Z_BLOCK:(Q_BLOCK + 1) * Z_BLOCK], (1, 0, 2))
    o = attention(q, mem_k, mem_v, att_sb, att_tq)
    o_tm = jnp.transpose(o, (1, 0, 2))
    x1, nsa, nsb, hl = mix(z3, o_tm, x_tm, sa_tm, sb_tm, h0, w, tt, nb)
    return x1.reshape(m, D_MODEL), nsa, nsb, hl


def _to_tm(a):
    return jnp.transpose(a, (1, 0, 2))


def kernel(x_prompt, x_sample, state_conv_a, state_conv_b, state_rglru, cache_mem_k, cache_mem_v, mem_prompt, w_in, b_in, w_conv_a, b_conv_a, ln_a_g, ln_a_b, w_a_out, w_conv_b, b_conv_b, w_rg_a, b_rg_a, w_rg_x, b_rg_x, lru_lambda, w_b_out, w_mem_kv, w_c_out, w_mix_out, ln1_g, ln1_b, w_ff_gate, w_ff_up, w_ff_down, w_router, b_router, w_e_gate, w_e_up, w_e_down, ln2_g, ln2_b):
    p = dict(w_in=w_in, b_in=b_in, w_conv_a=w_conv_a, b_conv_a=b_conv_a, ln_a_g=ln_a_g,
             ln_a_b=ln_a_b, w_a_out=w_a_out, w_conv_b=w_conv_b, b_conv_b=b_conv_b,
             w_rg_a=w_rg_a, b_rg_a=b_rg_a, w_rg_x=w_rg_x, b_rg_x=b_rg_x, lru_lambda=lru_lambda,
             w_b_out=w_b_out, w_c_out=w_c_out, w_mix_out=w_mix_out, ln1_g=ln1_g, ln1_b=ln1_b,
             w_ff_gate=w_ff_gate, w_ff_up=w_ff_up, w_ff_down=w_ff_down, w_router=w_router,
             b_router=b_router, w_e_gate=w_e_gate, w_e_up=w_e_up, w_e_down=w_e_down,
             ln2_g=ln2_g, ln2_b=ln2_b)
    bp, tp, _ = x_prompt.shape
    bs, ts, _ = x_sample.shape
    n_mem = mem_prompt.shape[1]

    yp = _to_tm(x_prompt)
    ys = _to_tm(x_sample)
    zero_a = jnp.zeros((K_A - 1, bp, D_A), F32)
    zero_b = jnp.zeros((K_B - 1, bp, D_B), F32)
    zero_h = jnp.zeros((bp, D_B), F32)
    zero_bias = jnp.zeros((DEPTH, 1, 2 * D_C), F32)
    outs = {k: [] for k in ("pa", "pb", "ph", "pk", "pv", "sa", "sb", "sh")}
    for l in range(DEPTH):
        w = _layer_weights(l, p)
        kv = matmul_bias(mem_prompt.reshape(bp * n_mem, D_MODEL), w_mem_kv, zero_bias, l,
                         Z_BLOCK, "kv_proj").reshape(bp, n_mem, 2 * D_C)
        mk = kv[..., :D_C]
        mv = kv[..., D_C:]
        xp1, na, nb_, nh = _mix_layer(l, p, w, yp, zero_a, zero_b, zero_h, mk, mv,
                                      tt=32, nb=8, att_sb=1, att_tq=512)
        outs["pa"].append(_to_tm(na))
        outs["pb"].append(_to_tm(nb_))
        outs["ph"].append(nh)
        outs["pk"].append(mk.reshape(bp, n_mem, H_C, DH_C))
        outs["pv"].append(mv.reshape(bp, n_mem, H_C, DH_C))
        xs1, na, nb_, nh = _mix_layer(l, p, w, ys, _to_tm(state_conv_a[l]),
                                      _to_tm(state_conv_b[l]), state_rglru[l],
                                      cache_mem_k[l].reshape(bs, n_mem, D_C),
                                      cache_mem_v[l].reshape(bs, n_mem, D_C),
                                      tt=ts, nb=32, att_sb=8, att_tq=ts)
        outs["sa"].append(_to_tm(na))
        outs["sb"].append(_to_tm(nb_))
        outs["sh"].append(nh)
        if l % 2 == 0:
            j = l // 2
            xp2, xs2 = [ffn_dense(x, w_ff_gate, w_ff_up, w_ff_down, ln2_g[l], ln2_b[l], j)
                        for x in (xp1, xs1)]
        else:
            xp2, xs2 = moe_routed([xp1, xs1], p, l)
        yp = xp2.reshape(tp, bp, D_MODEL)
        ys = xs2.reshape(ts, bs, D_MODEL)

    st = lambda k: jnp.stack(outs[k])
    return (_to_tm(yp), _to_tm(ys), st("pa"), st("pb"), st("ph"), st("pk"), st("pv"),
            st("sa"), st("sb"), st("sh"))
```

```python
import functools

import jax
import jax.numpy as jnp
from jax import lax
from jax.experimental import pallas as pl
from jax.experimental.pallas import tpu as pltpu

D_MODEL = 1024
N_MEM = 256
D_A = 512
K_A = 31
D_B = 1024
K_B = 4
H_B = 8
BW_B = D_B // H_B
LRU_C = 8.0
D_C = 512
H_C = 4
DH_C = D_C // H_C
N_BRANCH = 3
D_IN = 2 * D_A + 2 * D_B + D_C + N_BRANCH * D_MODEL
D_FF = 2816
N_EXPERTS = 8
TOP_K = 2
DEPTH = 2
ALPHA = (2.0 * DEPTH) ** 0.25
LN_EPS = 1e-5

Z_BLOCK = 512
Q_BLOCK = (2 * D_A + 2 * D_B) // Z_BLOCK
ZM_BLOCK = Q_BLOCK + 1

LANES = 128
VMEM_LIMIT = 56 * 1024 * 1024

TM = 1024
TF = 256
TOK = D_MODEL // LANES
assert TOK == 8

BF16 = jnp.bfloat16
F32 = jnp.float32


def _params(*sem):
    return pltpu.CompilerParams(dimension_semantics=sem, vmem_limit_bytes=VMEM_LIMIT)


def _layer_norm(x, g, b):
    mu = jnp.mean(x, axis=-1, keepdims=True)
    xc = x - mu
    var = jnp.mean(xc * xc, axis=-1, keepdims=True)
    return xc * lax.rsqrt(var + LN_EPS) * g + b


def _bdot(a, b):
    return jnp.dot(a.astype(BF16), b.astype(BF16), preferred_element_type=F32)


def _matmul_bias_kernel(x_ref, w_ref, b_ref, o_ref, xb_ref):
    @pl.when(pl.program_id(1) == 0)
    def _():
        xb_ref[...] = x_ref[...].astype(BF16)

    o_ref[...] = jnp.dot(xb_ref[...], w_ref[...].astype(BF16),
                         preferred_element_type=F32) + b_ref[...]


def matmul_bias(x, w, b, l, tn, name):
    m, k = x.shape
    n = w.shape[2]
    return pl.pallas_call(
        _matmul_bias_kernel,
        grid=(m // TM, n // tn),
        in_specs=[pl.BlockSpec((TM, k), lambda i, j: (i, 0)),
                  pl.BlockSpec((None, k, tn), lambda i, j: (l, 0, j)),
                  pl.BlockSpec((None, 1, tn), lambda i, j: (l, 0, j))],
        out_specs=pl.BlockSpec((TM, tn), lambda i, j: (i, j)),
        out_shape=jax.ShapeDtypeStruct((m, n), F32),
        scratch_shapes=[pltpu.VMEM((TM, k), BF16)],
        compiler_params=_params("arbitrary", "arbitrary"),
        name=name,
    )(x, w, b)


def _attention_kernel(q_ref, k_ref, v_ref, o_ref, *, sb):
    for s in range(sb):
        for h in range(H_C):
            cols = slice(h * DH_C, (h + 1) * DH_C)
            qh = q_ref[s, :, cols].astype(BF16)
            kh = k_ref[s, :, cols].astype(BF16)
            vh = v_ref[s, :, cols].astype(BF16)
            sc = lax.dot_general(qh, kh, (((1,), (1,)), ((), ())),
                                 preferred_element_type=F32) * (DH_C ** -0.5)
            e = jnp.exp(sc - jnp.max(sc, axis=-1, keepdims=True))
            p = e * (1.0 / jnp.sum(e, axis=-1, keepdims=True))
            o_ref[s, :, cols] = jnp.dot(p.astype(BF16), vh, preferred_element_type=F32)


def attention(q, k, v, kv_index, sb, tq):
    s, t, _ = q.shape
    lk, kc, vc = kv_index
    return pl.pallas_call(
        functools.partial(_attention_kernel, sb=sb),
        grid=(s // sb, t // tq),
        in_specs=[pl.BlockSpec((sb, tq, D_C), lambda i, j: (i, j, 0)),
                  pl.BlockSpec((None, sb, N_MEM, D_C), lambda i, j: (lk, i, 0, kc)),
                  pl.BlockSpec((None, sb, N_MEM, D_C), lambda i, j: (lk, i, 0, vc))],
        out_specs=pl.BlockSpec((sb, tq, D_C), lambda i, j: (i, j, 0)),
        out_shape=jax.ShapeDtypeStruct((s, t, D_C), F32),
        compiler_params=_params("arbitrary", "arbitrary"),
        name="attention",
    )(q, k, v)


def _mix_kernel(za_ref, zb_ref, zg_ref, m0a_ref, m0b_ref, m1a_ref, m1b_ref, m2a_ref, m2b_ref,
                o_ref, x_ref, sa_ref, sb_ref, h0_ref,
                wca_ref, bca_ref, lag_ref, lab_ref, wao_ref,
                wcb_ref, bcb_ref, wrg_ref, bra_ref, brx_ref, lam_ref, wbo_ref,
                wco_ref, wmo_ref, l1g_ref, l1b_ref,
                x1_ref, nsa_ref, nsb_ref, hl_ref,
                ua_buf, xb_buf, a_buf, u_buf, hs_buf, h_carry, *, tt, nb, nt):
    r = tt * nb
    ha = (K_A - 1) * nb
    hb = (K_B - 1) * nb
    i = pl.program_id(1)

    @pl.when(i == 0)
    def _():
        ua_buf[0:ha, :] = sa_ref[...].reshape(ha, D_A)
        xb_buf[0:hb, :] = sb_ref[...].reshape(hb, D_B)
        h_carry[...] = h0_ref[...]

    za = za_ref[...].reshape(r, 2 * D_A)
    ua_buf[ha:ha + r, :] = za[:, :D_A] * jax.nn.sigmoid(za[:, D_A:])
    c_a = jnp.broadcast_to(bca_ref[...], (r, D_A))
    for k in range(K_A):
        c_a = c_a + wca_ref[k:k + 1, :] * ua_buf[k * nb:k * nb + r, :]
    y_a = _bdot(jax.nn.silu(_layer_norm(c_a, lag_ref[...], lab_ref[...])), wao_ref[...])
    nsa_ref[...] = ua_buf[r:r + ha, :].reshape(K_A - 1, nb, D_A)

    xb_buf[hb:hb + r, :] = zb_ref[...].reshape(r, D_B)
    c_b = jnp.broadcast_to(bcb_ref[...], (r, D_B))
    for k in range(K_B):
        c_b = c_b + wcb_ref[k:k + 1, :] * xb_buf[k * nb:k * nb + r, :]
    nsb_ref[...] = xb_buf[r:r + hb, :].reshape(K_B - 1, nb, D_B)
    c_bf = c_b.astype(BF16)
    ra, rx = [], []
    for h in range(H_B):
        gh = jnp.dot(c_bf[:, h * BW_B:(h + 1) * BW_B], wrg_ref[h], preferred_element_type=F32)
        ra.append(gh[:, :BW_B])
        rx.append(gh[:, BW_B:])
    gate_r = jax.nn.sigmoid(jnp.concatenate(ra, axis=1) + bra_ref[...])
    gate_i = jax.nn.sigmoid(jnp.concatenate(rx, axis=1) + brx_ref[...])
    log_a = (-LRU_C) * gate_r * jax.nn.softplus(-lam_ref[...])
    a_buf[...] = jnp.exp(log_a)
    th = jnp.tanh(log_a)
    u_buf[...] = jnp.sqrt(-2.0 * th / (1.0 - th)) * (gate_i * c_b)
    h = h_carry[...]
    for t in range(tt):
        rows = slice(t * nb, (t + 1) * nb)
        h = a_buf[rows, :] * h + u_buf[rows, :]
        hs_buf[rows, :] = h
    h_carry[...] = h
    hl_ref[...] = h
    y_b = _bdot(hs_buf[...] * jax.nn.gelu(zg_ref[...].reshape(r, D_B)), wbo_ref[...])

    if nt > 1:
        ua_buf[0:ha, :] = ua_buf[r:r + ha, :]
        xb_buf[0:hb, :] = xb_buf[r:r + hb, :]

    y_c = _bdot(o_ref[...].reshape(r, D_C), wco_ref[...])

    def gate(ref):
        return jax.nn.sigmoid(ref[...].reshape(r, Z_BLOCK))

    half = D_MODEL // 2
    lo = (gate(m0a_ref) * y_a[:, :half] + gate(m1a_ref) * y_b[:, :half]
          + gate(m2a_ref) * y_c[:, :half])
    hi = (gate(m0b_ref) * y_a[:, half:] + gate(m1b_ref) * y_b[:, half:]
          + gate(m2b_ref) * y_c[:, half:])
    mix = _bdot(jnp.concatenate([lo, hi], axis=1), wmo_ref[...])
    x = x_ref[...].reshape(r, D_MODEL)
    x1_ref[...] = _layer_norm(ALPHA * x + mix, l1g_ref[...], l1b_ref[...]).reshape(tt, nb, D_MODEL)


def mix(z3, o_tm, x_tm, sa_tm, sb_tm, h0, w, tt, nb):
    t, nbt, _ = x_tm.shape
    nt = t // tt
    assert nt == 1 or tt >= K_A - 1
    r = tt * nb
    grid = (nbt // nb, nt)

    def zspec(width, blk):
        return pl.BlockSpec((tt, nb, width), lambda g, i: (i, g, blk))

    def full(a):
        nd = a.ndim
        return pl.BlockSpec(a.shape, lambda g, i: (0,) * nd)

    weights = [w["w_conv_a"], w["b_conv_a"], w["ln_a_g"], w["ln_a_b"], w["w_a_out"],
               w["w_conv_b"], w["b_conv_b"], w["w_rg"], w["b_rg_a"], w["b_rg_x"], w["lam"],
               w["w_b_out"], w["w_c_out"], w["w_mix_out"], w["ln1_g"], w["ln1_b"]]
    in_specs = ([zspec(2 * D_A, 0), zspec(D_B, 1), zspec(D_B, 2)]
                + [zspec(Z_BLOCK, ZM_BLOCK + j) for j in range(6)]
                + [zspec(D_C, 0), zspec(D_MODEL, 0),
                   pl.BlockSpec((K_A - 1, nb, D_A), lambda g, i: (0, g, 0)),
                   pl.BlockSpec((K_B - 1, nb, D_B), lambda g, i: (0, g, 0)),
                   pl.BlockSpec((nb, D_B), lambda g, i: (g, 0))]
                + [full(a) for a in weights])
    out_specs = [zspec(D_MODEL, 0),
                 pl.BlockSpec((K_A - 1, nb, D_A), lambda g, i: (0, g, 0)),
                 pl.BlockSpec((K_B - 1, nb, D_B), lambda g, i: (0, g, 0)),
                 pl.BlockSpec((nb, D_B), lambda g, i: (g, 0))]
    out_shape = [jax.ShapeDtypeStruct((t, nbt, D_MODEL), F32),
                 jax.ShapeDtypeStruct((K_A - 1, nbt, D_A), F32),
                 jax.ShapeDtypeStruct((K_B - 1, nbt, D_B), F32),
                 jax.ShapeDtypeStruct((nbt, D_B), F32)]
    scratch = [pltpu.VMEM(((K_A - 1) * nb + r, D_A), F32),
               pltpu.VMEM(((K_B - 1) * nb + r, D_B), F32),
               pltpu.VMEM((r, D_B), F32), pltpu.VMEM((r, D_B), F32), pltpu.VMEM((r, D_B), F32),
               pltpu.VMEM((nb, D_B), F32)]
    return pl.pallas_call(
        functools.partial(_mix_kernel, tt=tt, nb=nb, nt=nt),
        grid=grid, in_specs=in_specs, out_specs=out_specs, out_shape=out_shape,
        scratch_shapes=scratch,
        compiler_params=_params("arbitrary", "arbitrary"),
        name="mix",
    )(*([z3] * 9), o_tm, x_tm, sa_tm, sb_tm, h0, *weights)


def _ffn_kernel(x_ref, wg_ref, wu_ref, wd_ref, g_ref, b_ref, o_ref, xb_ref, acc_ref):
    j = pl.program_id(1)

    @pl.when(j == 0)
    def _():
        xb_ref[...] = x_ref[...].astype(BF16)
        acc_ref[...] = jnp.zeros_like(acc_ref)

    xb = xb_ref[...]
    hg = jnp.dot(xb, wg_ref[...].astype(BF16), preferred_element_type=F32)
    hu = jnp.dot(xb, wu_ref[...].astype(BF16), preferred_element_type=F32)
    acc_ref[...] += _bdot(jax.nn.silu(hg) * hu, wd_ref[...])

    @pl.when(j == pl.num_programs(1) - 1)
    def _():
        o_ref[...] = _layer_norm(ALPHA * x_ref[...] + acc_ref[...], g_ref[...], b_ref[...])


def ffn_dense(x, wg, wu, wd, g, b, l):
    m = x.shape[0]
    return pl.pallas_call(
        _ffn_kernel,
        grid=(m // TM, D_FF // TF),
        in_specs=[pl.BlockSpec((TM, D_MODEL), lambda i, j: (i, 0)),
                  pl.BlockSpec((None, D_MODEL, TF), lambda i, j: (l, 0, j)),
                  pl.BlockSpec((None, D_MODEL, TF), lambda i, j: (l, 0, j)),
                  pl.BlockSpec((None, TF, D_MODEL), lambda i, j: (l, j, 0)),
                  pl.BlockSpec((1, D_MODEL), lambda i, j: (0, 0)),
                  pl.BlockSpec((1, D_MODEL), lambda i, j: (0, 0))],
        out_specs=pl.BlockSpec((TM, D_MODEL), lambda i, j: (i, 0)),
        out_shape=jax.ShapeDtypeStruct((m, D_MODEL), F32),
        scratch_shapes=[pltpu.VMEM((TM, D_MODEL), BF16), pltpu.VMEM((TM, D_MODEL), F32)],
        compiler_params=_params("arbitrary", "arbitrary"),
        name="ffn_dense",
    )(x, wg, wu, wd, g.reshape(1, -1), b.reshape(1, -1))


def _router_kernel(x_ref, w_ref, b_ref, sel_ref, prob_ref):
    logits = jnp.dot(x_ref[...], w_ref[...], precision=lax.Precision.HIGHEST,
                     preferred_element_type=F32) + b_ref[...]
    lane = lax.broadcasted_iota(jnp.int32, logits.shape, 1)
    neg = jnp.float32(-jnp.inf)
    l1 = jnp.where(lane < N_EXPERTS, logits, neg)
    m1 = jnp.max(l1, axis=1, keepdims=True)
    i1 = jnp.min(jnp.where(l1 == m1, lane, LANES), axis=1, keepdims=True)
    l2 = jnp.where(lane == i1, neg, l1)
    m2 = jnp.max(l2, axis=1, keepdims=True)
    i2 = jnp.min(jnp.where(l2 == m2, lane, LANES), axis=1, keepdims=True)
    e2 = jnp.exp(m2 - m1)
    den = 1.0 + e2
    sel_ref[...] = jnp.where(lane == 0, i1, jnp.where(lane == 1, i2, 0))
    prob_ref[...] = jnp.where(lane == 0, 1.0 / den, jnp.where(lane == 1, e2 / den, 0.0))


def router(x, w_router, b_router):
    m = x.shape[0]
    wp = jnp.zeros((D_MODEL, LANES), F32).at[:, :N_EXPERTS].set(w_router)
    bp = jnp.zeros((1, LANES), F32).at[0, :N_EXPERTS].set(b_router)
    return pl.pallas_call(
        _router_kernel,
        grid=(m // TM,),
        in_specs=[pl.BlockSpec((TM, D_MODEL), lambda i: (i, 0)),
                  pl.BlockSpec((D_MODEL, LANES), lambda i: (0, 0)),
                  pl.BlockSpec((1, LANES), lambda i: (0, 0))],
        out_specs=[pl.BlockSpec((TM, LANES), lambda i: (i, 0)),
                   pl.BlockSpec((TM, LANES), lambda i: (i, 0))],
        out_shape=[jax.ShapeDtypeStruct((m, LANES), jnp.int32),
                   jax.ShapeDtypeStruct((m, LANES), F32)],
        compiler_params=_params("arbitrary"),
        name="router",
    )(x, wp, bp)


def _route(sel, tm, n_tiles):
    e = sel[:, :TOP_K].reshape(-1)
    onehot = (e[:, None] == jnp.arange(N_EXPERTS, dtype=jnp.int32)[None, :]).astype(jnp.int32)
    csum = jnp.cumsum(onehot, axis=0)
    rank = jnp.sum(onehot * csum, axis=1) - 1
    counts = csum[-1]
    padded = ((counts + tm - 1) // tm) * tm
    ends = jnp.cumsum(padded)
    starts = ends - padded
    dest = jnp.sum(onehot * starts[None, :], axis=1) + rank
    first_row = jnp.arange(n_tiles, dtype=jnp.int32) * tm
    tile_expert = jnp.sum((first_row[:, None] >= ends[None, :]).astype(jnp.int32), axis=1)
    tile_expert = jnp.minimum(tile_expert, N_EXPERTS - 1)
    n_used = (ends[-1] // tm).reshape(1)
    return dest.astype(jnp.int32), tile_expert.astype(jnp.int32), n_used.astype(jnp.int32)


def _token_copy(src_ref, dst_ref, sem, s_tok, d_tok, n_tok):
    s0 = pl.multiple_of(s_tok * TOK, TOK)
    d0 = pl.multiple_of(d_tok * TOK, TOK)
    return pltpu.make_async_copy(src_ref.at[pl.ds(s0, n_tok * TOK), :],
                                 dst_ref.at[pl.ds(d0, n_tok * TOK), :], sem)


def _to_token_tiles(dst_ref, src, rows):
    for s in range(TOK):
        dst_ref[pl.ds(s, rows, stride=TOK), :] = src[:, s * LANES:(s + 1) * LANES]


def _dispatch_kernel(didx_ref, x_ref, init_ref, out_ref, tok_buf, sem):
    del init_ref
    _to_token_tiles(tok_buf, x_ref[...], TM)

    def issue(t, carry):
        for k in range(TOP_K):
            _token_copy(tok_buf, out_ref, sem, t, didx_ref[0, TOP_K * t + k], 1).start()
        return carry

    lax.fori_loop(0, TM, issue, 0)
    for _ in range(TOP_K):
        _token_copy(tok_buf, out_ref, sem, 0, 0, TM).wait()


def dispatch(x, dest, grouped):
    m = x.shape[0]
    nc = m // TM
    return pl.pallas_call(
        _dispatch_kernel,
        grid=(nc,),
        in_specs=[pl.BlockSpec((None, 1, TOP_K * TM), lambda c: (c, 0, 0),
                               memory_space=pltpu.SMEM),
                  pl.BlockSpec((TM, D_MODEL), lambda c: (c, 0)),
                  pl.BlockSpec(memory_space=pl.ANY)],
        out_specs=pl.BlockSpec(memory_space=pl.ANY),
        out_shape=jax.ShapeDtypeStruct(grouped.shape, grouped.dtype),
        scratch_shapes=[pltpu.VMEM((TM * TOK, LANES), F32), pltpu.SemaphoreType.DMA(())],
        input_output_aliases={2: 0},
        compiler_params=_params("arbitrary"),
        name="dispatch",
    )(dest.reshape(nc, 1, TOP_K * TM), x, grouped)


def _gmm_kernel(te_ref, nu_ref, x_ref, wg_ref, wu_ref, wd_ref, y_ref, xb_ref, acc_ref):
    del te_ref
    i = pl.program_id(0)
    j = pl.program_id(1)

    @pl.when((i >= nu_ref[0]) & (j == 0))
    def _():
        y_ref[...] = jnp.zeros_like(y_ref)

    @pl.when(i < nu_ref[0])
    def _():
        @pl.when(j == 0)
        def _():
            for s in range(TOK):
                xb_ref[:, s * LANES:(s + 1) * LANES] = (
                    x_ref[pl.ds(s, TM, stride=TOK), :].astype(BF16))

        xb = xb_ref[...]
        hg = jnp.dot(xb, wg_ref[...].astype(BF16), preferred_element_type=F32)
        hu = jnp.dot(xb, wu_ref[...].astype(BF16), preferred_element_type=F32)
        y = _bdot(jax.nn.silu(hg) * hu, wd_ref[...])

        @pl.when(j == 0)
        def _():
            acc_ref[...] = y

        @pl.when(j > 0)
        def _():
            acc_ref[...] += y

        @pl.when(j == pl.num_programs(1) - 1)
        def _():
            _to_token_tiles(y_ref, acc_ref, TM)


def gmm(xs, tile_expert, n_used, wg, wu, wd, l):
    rows = xs.shape[0] // TOK
    n_tiles = rows // TM
    nf = D_FF // TF

    def tile(i, nu):
        return jnp.minimum(i, nu[0] - 1)

    def ff(i, j, nu):
        return jnp.where(i < nu[0], j, nf - 1)

    grid_spec = pltpu.PrefetchScalarGridSpec(
        num_scalar_prefetch=2,
        grid=(n_tiles, nf),
        in_specs=[pl.BlockSpec((TM * TOK, LANES), lambda i, j, te, nu: (i, 0)),
                  pl.BlockSpec((None, None, D_MODEL, TF),
                               lambda i, j, te, nu: (l, te[tile(i, nu)], 0, ff(i, j, nu))),
                  pl.BlockSpec((None, None, D_MODEL, TF),
                               lambda i, j, te, nu: (l, te[tile(i, nu)], 0, ff(i, j, nu))),
                  pl.BlockSpec((None, None, TF, D_MODEL),
                               lambda i, j, te, nu: (l, te[tile(i, nu)], ff(i, j, nu), 0))],
        out_specs=pl.BlockSpec((TM * TOK, LANES), lambda i, j, te, nu: (i, 0)),
        scratch_shapes=[pltpu.VMEM((TM, D_MODEL), BF16), pltpu.VMEM((TM, D_MODEL), F32)],
    )
    return pl.pallas_call(
        _gmm_kernel,
        grid_spec=grid_spec,
        out_shape=jax.ShapeDtypeStruct(xs.shape, F32),
        compiler_params=_params("arbitrary", "arbitrary"),
        name="gmm",
    )(tile_expert, n_used, xs, wg, wu, wd)


def _combine_kernel(didx_ref, x_ref, prob_ref, g_ref, b_ref, y_ref, o_ref, ybuf, sem):
    def issue(t, carry):
        for k in range(TOP_K):
            _token_copy(y_ref, ybuf, sem, didx_ref[0, TOP_K * t + k], k * TM + t, 1).start()
        return carry

    lax.fori_loop(0, TM, issue, 0)
    for k in range(TOP_K):
        _token_copy(y_ref, ybuf, sem, 0, k * TM, TM).wait()
    p0 = prob_ref[:, 0:1]
    p1 = prob_ref[:, 1:2]
    f = jnp.concatenate(
        [p0 * ybuf[pl.ds(s, TM, stride=TOK), :] + p1 * ybuf[pl.ds(TM * TOK + s, TM, stride=TOK), :]
         for s in range(TOK)], axis=1)
    o_ref[...] = _layer_norm(ALPHA * x_ref[...] + f, g_ref[...], b_ref[...])


def combine(x, y, dest, prob, g, b):
    m = x.shape[0]
    nc = m // TM
    return pl.pallas_call(
        _combine_kernel,
        grid=(nc,),
        in_specs=[pl.BlockSpec((None, 1, TOP_K * TM), lambda i: (i, 0, 0),
                               memory_space=pltpu.SMEM),
                  pl.BlockSpec((TM, D_MODEL), lambda i: (i, 0)),
                  pl.BlockSpec((TM, LANES), lambda i: (i, 0)),
                  pl.BlockSpec((1, D_MODEL), lambda i: (0, 0)),
                  pl.BlockSpec((1, D_MODEL), lambda i: (0, 0)),
                  pl.BlockSpec(memory_space=pl.ANY)],
        out_specs=pl.BlockSpec((TM, D_MODEL), lambda i: (i, 0)),
        out_shape=jax.ShapeDtypeStruct((m, D_MODEL), F32),
        scratch_shapes=[pltpu.VMEM((TOP_K * TM * TOK, LANES), F32),
                        pltpu.SemaphoreType.DMA(())],
        compiler_params=_params("arbitrary"),
        name="combine",
    )(dest.reshape(nc, 1, TOP_K * TM), x, prob, g.reshape(1, -1), b.reshape(1, -1), y)


def moe_routed(xs_list, p, l):
    j = l // 2
    routed = [router(x, p["w_router"][j], p["b_router"][j]) for x in xs_list]
    sel = jnp.concatenate([r[0] for r in routed], axis=0)
    n_pairs = TOP_K * sel.shape[0]
    n_tiles = n_pairs // TM + N_EXPERTS
    dest, tile_expert, n_used = _route(sel, TM, n_tiles)
    bounds = [0]
    for x in xs_list:
        bounds.append(bounds[-1] + TOP_K * x.shape[0])

    grouped = jnp.zeros((n_tiles * TM * TOK, LANES), F32)
    for x, lo, hi in zip(xs_list, bounds[:-1], bounds[1:]):
        grouped = dispatch(x, dest[lo:hi], grouped)
    y = gmm(grouped, tile_expert, n_used, p["w_e_gate"], p["w_e_up"], p["w_e_down"], j)
    return [combine(x, y, dest[lo:hi], prob, p["ln2_g"][l], p["ln2_b"][l])
            for x, (_, prob), lo, hi in zip(xs_list, routed, bounds[:-1], bounds[1:])]


def _layer_weights(l, p):
    row = lambda a: a[l].reshape(1, -1)
    return {
        "w_conv_a": p["w_conv_a"][l], "b_conv_a": row(p["b_conv_a"]),
        "ln_a_g": row(p["ln_a_g"]), "ln_a_b": row(p["ln_a_b"]),
        "w_a_out": p["w_a_out"][l].astype(BF16),
        "w_conv_b": p["w_conv_b"][l], "b_conv_b": row(p["b_conv_b"]),
        "w_rg": jnp.concatenate([p["w_rg_a"][l], p["w_rg_x"][l]], axis=-1).astype(BF16),
        "b_rg_a": row(p["b_rg_a"]), "b_rg_x": row(p["b_rg_x"]), "lam": row(p["lru_lambda"]),
        "w_b_out": p["w_b_out"][l].astype(BF16), "w_c_out": p["w_c_out"][l].astype(BF16),
        "w_mix_out": p["w_mix_out"][l].astype(BF16),
        "ln1_g": row(p["ln1_g"]), "ln1_b": row(p["ln1_b"]),
    }


def _mix_layer(l, p, w, x_tm, sa_tm, sb_tm, h0, mem_k, mem_v, kv_index, tt, nb, att_sb, att_tq):
    t, nbt, _ = x_tm.shape
    m = t * nbt
    z = matmul_bias(x_tm.reshape(m, D_MODEL), p["w_in"], p["b_in"].reshape(DEPTH, 1, D_IN), l,
                    Z_BLOCK, "in_proj")
    z3 = z.reshape(t, nbt, D_IN)
    q = jnp.transpose(z3[:, :, Q_BLOCK * Z_BLOCK:(Q_BLOCK + 1) * Z_BLOCK], (1, 0, 2))
    o = attention(q, mem_k, mem_v, kv_index, att_sb, att_tq)
    o_tm = jnp.transpose(o, (1, 0, 2))
    x1, nsa, nsb, hl = mix(z3, o_tm, x_tm, sa_tm, sb_tm, h0, w, tt, nb)
    return x1.reshape(m, D_MODEL), nsa, nsb, hl


def _to_tm(a):
    return jnp.transpose(a, (1, 0, 2))


def kernel(x_prompt, x_sample, state_conv_a, state_conv_b, state_rglru, cache_mem_k, cache_mem_v, mem_prompt, w_in, b_in, w_conv_a, b_conv_a, ln_a_g, ln_a_b, w_a_out, w_conv_b, b_conv_b, w_rg_a, b_rg_a, w_rg_x, b_rg_x, lru_lambda, w_b_out, w_mem_kv, w_c_out, w_mix_out, ln1_g, ln1_b, w_ff_gate, w_ff_up, w_ff_down, w_router, b_router, w_e_gate, w_e_up, w_e_down, ln2_g, ln2_b):
    p = dict(w_in=w_in, b_in=b_in, w_conv_a=w_conv_a, b_conv_a=b_conv_a, ln_a_g=ln_a_g,
             ln_a_b=ln_a_b, w_a_out=w_a_out, w_conv_b=w_conv_b, b_conv_b=b_conv_b,
             w_rg_a=w_rg_a, b_rg_a=b_rg_a, w_rg_x=w_rg_x, b_rg_x=b_rg_x, lru_lambda=lru_lambda,
             w_b_out=w_b_out, w_c_out=w_c_out, w_mix_out=w_mix_out, ln1_g=ln1_g, ln1_b=ln1_b,
             w_ff_gate=w_ff_gate, w_ff_up=w_ff_up, w_ff_down=w_ff_down, w_router=w_router,
             b_router=b_router, w_e_gate=w_e_gate, w_e_up=w_e_up, w_e_down=w_e_down,
             ln2_g=ln2_g, ln2_b=ln2_b)
    bp, tp, _ = x_prompt.shape
    bs, ts, _ = x_sample.shape
    n_mem = mem_prompt.shape[1]

    yp = _to_tm(x_prompt)
    ys = _to_tm(x_sample)
    zero_a = jnp.zeros((K_A - 1, bp, D_A), F32)
    zero_b = jnp.zeros((K_B - 1, bp, D_B), F32)
    zero_h = jnp.zeros((bp, D_B), F32)
    zero_bias = jnp.zeros((DEPTH, 1, 2 * D_C), F32)
    cache_k = cache_mem_k.reshape(DEPTH, bs, n_mem, D_C)
    cache_v = cache_mem_v.reshape(DEPTH, bs, n_mem, D_C)
    outs = {k: [] for k in ("pa", "pb", "ph", "pk", "pv", "sa", "sb", "sh")}
    for l in range(DEPTH):
        w = _layer_weights(l, p)
        kv = matmul_bias(mem_prompt.reshape(bp * n_mem, D_MODEL), w_mem_kv, zero_bias, l,
                         Z_BLOCK, "kv_proj").reshape(bp, n_mem, 2 * D_C)
        mk = kv[..., :D_C]
        mv = kv[..., D_C:]
        kv4 = kv.reshape(1, bp, n_mem, 2 * D_C)
        xp1, na, nb_, nh = _mix_layer(l, p, w, yp, zero_a, zero_b, zero_h, kv4, kv4, (0, 0, 1),
                                      tt=32, nb=8, att_sb=1, att_tq=512)
        outs["pa"].append(_to_tm(na))
        outs["pb"].append(_to_tm(nb_))
        outs["ph"].append(nh)
        outs["pk"].append(mk.reshape(bp, n_mem, H_C, DH_C))
        outs["pv"].append(mv.reshape(bp, n_mem, H_C, DH_C))
        xs1, na, nb_, nh = _mix_layer(l, p, w, ys, _to_tm(state_conv_a[l]),
                                      _to_tm(state_conv_b[l]), state_rglru[l],
                                      cache_k, cache_v, (l, 0, 0),
                                      tt=ts, nb=32, att_sb=8, att_tq=ts)
        outs["sa"].append(_to_tm(na))
        outs["sb"].append(_to_tm(nb_))
        outs["sh"].append(nh)
        if l % 2 == 0:
            j = l // 2
            xp2, xs2 = [ffn_dense(x, w_ff_gate, w_ff_up, w_ff_down, ln2_g[l], ln2_b[l], j)
                        for x in (xp1, xs1)]
        else:
            xp2, xs2 = moe_routed([xp1, xs1], p, l)
        yp = xp2.reshape(tp, bp, D_MODEL)
        ys = xs2.reshape(ts, bs, D_MODEL)

    st = lambda k: jnp.stack(outs[k])
    return (_to_tm(yp), _to_tm(ys), st("pa"), st("pb"), st("ph"), st("pk"), st("pv"),
            st("sa"), st("sb"), st("sh"))
```

```python
import functools

import jax
import jax.numpy as jnp
from jax import lax
from jax.experimental import pallas as pl
from jax.experimental.pallas import tpu as pltpu

D_MODEL = 1024
N_MEM = 256
D_A = 512
K_A = 31
D_B = 1024
K_B = 4
H_B = 8
BW_B = D_B // H_B
LRU_C = 8.0
D_C = 512
H_C = 4
DH_C = D_C // H_C
N_BRANCH = 3
D_IN = 2 * D_A + 2 * D_B + D_C + N_BRANCH * D_MODEL
D_FF = 2816
N_EXPERTS = 8
TOP_K = 2
DEPTH = 2
ALPHA = (2.0 * DEPTH) ** 0.25
LN_EPS = 1e-5

Z_BLOCK = 512
Q_BLOCK = (2 * D_A + 2 * D_B) // Z_BLOCK
ZM_BLOCK = Q_BLOCK + 1

LANES = 128
VMEM_LIMIT = 56 * 1024 * 1024

TM = 1024
TF = 256
TOK = D_MODEL // LANES
assert TOK == 8

BF16 = jnp.bfloat16
F32 = jnp.float32


def _params(*sem):
    return pltpu.CompilerParams(dimension_semantics=sem, vmem_limit_bytes=VMEM_LIMIT)


def _layer_norm(x, g, b):
    mu = jnp.mean(x, axis=-1, keepdims=True)
    xc = x - mu
    var = jnp.mean(xc * xc, axis=-1, keepdims=True)
    return xc * lax.rsqrt(var + LN_EPS) * g + b


def _bdot(a, b):
    return jnp.dot(a.astype(BF16), b.astype(BF16), preferred_element_type=F32)


def _matmul_bias_kernel(x_ref, w_ref, b_ref, o_ref, *rest, nb_seq):
    j = pl.program_id(1)
    xb_ref = rest[1] if nb_seq else rest[0]

    @pl.when(j == 0)
    def _():
        xb_ref[...] = x_ref[...].astype(BF16)

    o_ref[...] = jnp.dot(xb_ref[...], w_ref[...].astype(BF16),
                         preferred_element_type=F32) + b_ref[...]

    if nb_seq:
        q_ref, qs_ref = rest[0], rest[2]

        @pl.when(j == Q_BLOCK)
        def _():
            for c in range(D_C // LANES):
                qs_ref[c] = o_ref[:, c * LANES:(c + 1) * LANES]
            for s in range(nb_seq):
                for c in range(D_C // LANES):
                    q_ref[s, :, c * LANES:(c + 1) * LANES] = (
                        qs_ref[c, pl.ds(s, TM // nb_seq, stride=nb_seq), :])


def matmul_bias(x, w, b, l, tn, name, nb_seq=0):
    m, k = x.shape
    n = w.shape[2]
    out_specs = [pl.BlockSpec((TM, tn), lambda i, j: (i, j))]
    out_shape = [jax.ShapeDtypeStruct((m, n), F32)]
    scratch = [pltpu.VMEM((TM, k), BF16)]
    if nb_seq:
        assert tn == D_C
        out_specs.append(pl.BlockSpec((nb_seq, TM // nb_seq, D_C), lambda i, j: (0, i, 0)))
        out_shape.append(jax.ShapeDtypeStruct((nb_seq, m // nb_seq, D_C), F32))
        scratch.append(pltpu.VMEM((D_C // LANES, TM, LANES), F32))
    return pl.pallas_call(
        functools.partial(_matmul_bias_kernel, nb_seq=nb_seq),
        grid=(m // TM, n // tn),
        in_specs=[pl.BlockSpec((TM, k), lambda i, j: (i, 0)),
                  pl.BlockSpec((None, k, tn), lambda i, j: (l, 0, j)),
                  pl.BlockSpec((None, 1, tn), lambda i, j: (l, 0, j))],
        out_specs=out_specs,
        out_shape=out_shape,
        scratch_shapes=scratch,
        compiler_params=_params("arbitrary", "arbitrary"),
        name=name,
    )(x, w, b)


def _attention_kernel(q_ref, k_ref, v_ref, o_ref, *, sb):
    for s in range(sb):
        for h in range(H_C):
            cols = slice(h * DH_C, (h + 1) * DH_C)
            qh = q_ref[s, :, cols].astype(BF16)
            kh = k_ref[s, :, cols].astype(BF16)
            vh = v_ref[s, :, cols].astype(BF16)
            sc = lax.dot_general(qh, kh, (((1,), (1,)), ((), ())),
                                 preferred_element_type=F32) * (DH_C ** -0.5)
            e = jnp.exp(sc - jnp.max(sc, axis=-1, keepdims=True))
            p = e * (1.0 / jnp.sum(e, axis=-1, keepdims=True))
            o_ref[s, :, cols] = jnp.dot(p.astype(BF16), vh, preferred_element_type=F32)


def attention(q, k, v, kv_index, sb, tq):
    s, t, _ = q.shape
    lk, kc, vc = kv_index
    return pl.pallas_call(
        functools.partial(_attention_kernel, sb=sb),
        grid=(s // sb, t // tq),
        in_specs=[pl.BlockSpec((sb, tq, D_C), lambda i, j: (i, j, 0)),
                  pl.BlockSpec((None, sb, N_MEM, D_C), lambda i, j: (lk, i, 0, kc)),
                  pl.BlockSpec((None, sb, N_MEM, D_C), lambda i, j: (lk, i, 0, vc))],
        out_specs=pl.BlockSpec((sb, tq, D_C), lambda i, j: (i, j, 0)),
        out_shape=jax.ShapeDtypeStruct((s, t, D_C), F32),
        compiler_params=_params("arbitrary", "arbitrary"),
        name="attention",
    )(q, k, v)


def _attention_rows_kernel(q_ref, k_ref, v_ref, o_ref, *, sb, t):
    n_rows = H_C * t
    n_cols = N_MEM * H_C
    row_head = lax.broadcasted_iota(jnp.int32, (n_rows, n_cols), 0) // t
    col_head = lax.broadcasted_iota(jnp.int32, (n_rows, n_cols), 1) % H_C
    own = row_head == col_head
    for s in range(sb):
        q = q_ref[s]
        qa = jnp.concatenate([q[:, h * DH_C:(h + 1) * DH_C] for h in range(H_C)], axis=0)
        sc = lax.dot_general(qa.astype(BF16), k_ref[s].astype(BF16), (((1,), (1,)), ((), ())),
                             preferred_element_type=F32) * (DH_C ** -0.5)
        sc = jnp.where(own, sc, -jnp.inf)
        e = jnp.exp(sc - jnp.max(sc, axis=-1, keepdims=True))
        p = e * (1.0 / jnp.sum(e, axis=-1, keepdims=True))
        oa = jnp.dot(p.astype(BF16), v_ref[s].astype(BF16), preferred_element_type=F32)
        o_ref[s] = jnp.concatenate([oa[h * t:(h + 1) * t, :] for h in range(H_C)], axis=1)


def attention_rows(q, k, v, l, sb):
    s, t, _ = q.shape
    rows = N_MEM * H_C
    return pl.pallas_call(
        functools.partial(_attention_rows_kernel, sb=sb, t=t),
        grid=(s // sb,),
        in_specs=[pl.BlockSpec((sb, t, D_C), lambda i: (i, 0, 0)),
                  pl.BlockSpec((None, sb, rows, DH_C), lambda i: (l, i, 0, 0)),
                  pl.BlockSpec((None, sb, rows, DH_C), lambda i: (l, i, 0, 0))],
        out_specs=pl.BlockSpec((sb, t, D_C), lambda i: (i, 0, 0)),
        out_shape=jax.ShapeDtypeStruct((s, t, D_C), F32),
        compiler_params=_params("arbitrary"),
        name="attention_rows",
    )(q, k, v)


def _mix_kernel(za_ref, zb_ref, zg_ref, m0a_ref, m0b_ref, m1a_ref, m1b_ref, m2a_ref, m2b_ref,
                o_ref, x_ref, sa_ref, sb_ref, h0_ref,
                wca_ref, bca_ref, lag_ref, lab_ref, wao_ref,
                wcb_ref, bcb_ref, wrg_ref, bra_ref, brx_ref, lam_ref, wbo_ref,
                wco_ref, wmo_ref, l1g_ref, l1b_ref,
                x1_ref, nsa_ref, nsb_ref, hl_ref,
                ua_buf, xb_buf, a_buf, u_buf, hs_buf, h_carry, o_buf, *, tt, nb, nt, o_bm):
    r = tt * nb
    ha = (K_A - 1) * nb
    hb = (K_B - 1) * nb
    i = pl.program_id(1)

    @pl.when(i == 0)
    def _():
        ua_buf[0:ha, :] = sa_ref[...].reshape(ha, D_A)
        xb_buf[0:hb, :] = sb_ref[...].reshape(hb, D_B)
        h_carry[...] = h0_ref[...]

    za = za_ref[...].reshape(r, 2 * D_A)
    ua_buf[ha:ha + r, :] = za[:, :D_A] * jax.nn.sigmoid(za[:, D_A:])
    c_a = jnp.broadcast_to(bca_ref[...], (r, D_A))
    for k in range(K_A):
        c_a = c_a + wca_ref[k:k + 1, :] * ua_buf[k * nb:k * nb + r, :]
    y_a = _bdot(jax.nn.silu(_layer_norm(c_a, lag_ref[...], lab_ref[...])), wao_ref[...])
    nsa_ref[...] = ua_buf[r:r + ha, :].reshape(K_A - 1, nb, D_A)

    xb_buf[hb:hb + r, :] = zb_ref[...].reshape(r, D_B)
    c_b = jnp.broadcast_to(bcb_ref[...], (r, D_B))
    for k in range(K_B):
        c_b = c_b + wcb_ref[k:k + 1, :] * xb_buf[k * nb:k * nb + r, :]
    nsb_ref[...] = xb_buf[r:r + hb, :].reshape(K_B - 1, nb, D_B)
    c_bf = c_b.astype(BF16)
    ra, rx = [], []
    for h in range(H_B):
        gh = jnp.dot(c_bf[:, h * BW_B:(h + 1) * BW_B], wrg_ref[h], preferred_element_type=F32)
        ra.append(gh[:, :BW_B])
        rx.append(gh[:, BW_B:])
    gate_r = jax.nn.sigmoid(jnp.concatenate(ra, axis=1) + bra_ref[...])
    gate_i = jax.nn.sigmoid(jnp.concatenate(rx, axis=1) + brx_ref[...])
    log_a = (-LRU_C) * gate_r * jax.nn.softplus(-lam_ref[...])
    a_buf[...] = jnp.exp(log_a)
    th = jnp.tanh(log_a)
    u_buf[...] = jnp.sqrt(-2.0 * th / (1.0 - th)) * (gate_i * c_b)
    h = h_carry[...]
    for t in range(tt):
        rows = slice(t * nb, (t + 1) * nb)
        h = a_buf[rows, :] * h + u_buf[rows, :]
        hs_buf[rows, :] = h
    h_carry[...] = h
    hl_ref[...] = h
    y_b = _bdot(hs_buf[...] * jax.nn.gelu(zg_ref[...].reshape(r, D_B)), wbo_ref[...])

    if nt > 1:
        ua_buf[0:ha, :] = ua_buf[r:r + ha, :]
        xb_buf[0:hb, :] = xb_buf[r:r + hb, :]

    if o_bm:
        for s in range(nb):
            for c in range(D_C // LANES):
                o_buf[c, pl.ds(s, tt, stride=nb), :] = o_ref[s, :, c * LANES:(c + 1) * LANES]
        o_tm = jnp.concatenate([o_buf[c] for c in range(D_C // LANES)], axis=1)
    else:
        o_tm = o_ref[...].reshape(r, D_C)
    y_c = _bdot(o_tm, wco_ref[...])

    def gate(ref):
        return jax.nn.sigmoid(ref[...].reshape(r, Z_BLOCK))

    half = D_MODEL // 2
    lo = (gate(m0a_ref) * y_a[:, :half] + gate(m1a_ref) * y_b[:, :half]
          + gate(m2a_ref) * y_c[:, :half])
    hi = (gate(m0b_ref) * y_a[:, half:] + gate(m1b_ref) * y_b[:, half:]
          + gate(m2b_ref) * y_c[:, half:])
    mix = _bdot(jnp.concatenate([lo, hi], axis=1), wmo_ref[...])
    x = x_ref[...].reshape(r, D_MODEL)
    x1_ref[...] = _layer_norm(ALPHA * x + mix, l1g_ref[...], l1b_ref[...]).reshape(tt, nb, D_MODEL)


def mix(z3, o, x_tm, sa_tm, sb_tm, h0, w, tt, nb, o_bm):
    t, nbt, _ = x_tm.shape
    nt = t // tt
    assert nt == 1 or tt >= K_A - 1
    assert not o_bm or nb == nbt
    r = tt * nb
    grid = (nbt // nb, nt)
    o_spec = (pl.BlockSpec((nb, tt, D_C), lambda g, i: (0, i, 0)) if o_bm
              else pl.BlockSpec((tt, nb, D_C), lambda g, i: (i, g, 0)))

    def zspec(width, blk):
        return pl.BlockSpec((tt, nb, width), lambda g, i: (i, g, blk))

    def full(a):
        nd = a.ndim
        return pl.BlockSpec(a.shape, lambda g, i: (0,) * nd)

    weights = [w["w_conv_a"], w["b_conv_a"], w["ln_a_g"], w["ln_a_b"], w["w_a_out"],
               w["w_conv_b"], w["b_conv_b"], w["w_rg"], w["b_rg_a"], w["b_rg_x"], w["lam"],
               w["w_b_out"], w["w_c_out"], w["w_mix_out"], w["ln1_g"], w["ln1_b"]]
    in_specs = ([zspec(2 * D_A, 0), zspec(D_B, 1), zspec(D_B, 2)]
                + [zspec(Z_BLOCK, ZM_BLOCK + j) for j in range(6)]
                + [o_spec, zspec(D_MODEL, 0),
                   pl.BlockSpec((K_A - 1, nb, D_A), lambda g, i: (0, g, 0)),
                   pl.BlockSpec((K_B - 1, nb, D_B), lambda g, i: (0, g, 0)),
                   pl.BlockSpec((nb, D_B), lambda g, i: (g, 0))]
                + [full(a) for a in weights])
    out_specs = [zspec(D_MODEL, 0),
                 pl.BlockSpec((K_A - 1, nb, D_A), lambda g, i: (0, g, 0)),
                 pl.BlockSpec((K_B - 1, nb, D_B), lambda g, i: (0, g, 0)),
                 pl.BlockSpec((nb, D_B), lambda g, i: (g, 0))]
    out_shape = [jax.ShapeDtypeStruct((t, nbt, D_MODEL), F32),
                 jax.ShapeDtypeStruct((K_A - 1, nbt, D_A), F32),
                 jax.ShapeDtypeStruct((K_B - 1, nbt, D_B), F32),
                 jax.ShapeDtypeStruct((nbt, D_B), F32)]
    scratch = [pltpu.VMEM(((K_A - 1) * nb + r, D_A), F32),
               pltpu.VMEM(((K_B - 1) * nb + r, D_B), F32),
               pltpu.VMEM((r, D_B), F32), pltpu.VMEM((r, D_B), F32), pltpu.VMEM((r, D_B), F32),
               pltpu.VMEM((nb, D_B), F32), pltpu.VMEM((D_C // LANES, r, LANES), F32)]
    return pl.pallas_call(
        functools.partial(_mix_kernel, tt=tt, nb=nb, nt=nt, o_bm=o_bm),
        grid=grid, in_specs=in_specs, out_specs=out_specs, out_shape=out_shape,
        scratch_shapes=scratch,
        compiler_params=_params("arbitrary", "arbitrary"),
        name="mix",
    )(*([z3] * 9), o, x_tm, sa_tm, sb_tm, h0, *weights)


def _ffn_kernel(x_ref, wg_ref, wu_ref, wd_ref, g_ref, b_ref, o_ref, xb_ref, acc_ref):
    j = pl.program_id(1)

    @pl.when(j == 0)
    def _():
        xb_ref[...] = x_ref[...].astype(BF16)
        acc_ref[...] = jnp.zeros_like(acc_ref)

    xb = xb_ref[...]
    hg = jnp.dot(xb, wg_ref[...].astype(BF16), preferred_element_type=F32)
    hu = jnp.dot(xb, wu_ref[...].astype(BF16), preferred_element_type=F32)
    acc_ref[...] += _bdot(jax.nn.silu(hg) * hu, wd_ref[...])

    @pl.when(j == pl.num_programs(1) - 1)
    def _():
        o_ref[...] = _layer_norm(ALPHA * x_ref[...] + acc_ref[...], g_ref[...], b_ref[...])


def ffn_dense(x, wg, wu, wd, g, b, l):
    m = x.shape[0]
    return pl.pallas_call(
        _ffn_kernel,
        grid=(m // TM, D_FF // TF),
        in_specs=[pl.BlockSpec((TM, D_MODEL), lambda i, j: (i, 0)),
                  pl.BlockSpec((None, D_MODEL, TF), lambda i, j: (l, 0, j)),
                  pl.BlockSpec((None, D_MODEL, TF), lambda i, j: (l, 0, j)),
                  pl.BlockSpec((None, TF, D_MODEL), lambda i, j: (l, j, 0)),
                  pl.BlockSpec((1, D_MODEL), lambda i, j: (0, 0)),
                  pl.BlockSpec((1, D_MODEL), lambda i, j: (0, 0))],
        out_specs=pl.BlockSpec((TM, D_MODEL), lambda i, j: (i, 0)),
        out_shape=jax.ShapeDtypeStruct((m, D_MODEL), F32),
        scratch_shapes=[pltpu.VMEM((TM, D_MODEL), BF16), pltpu.VMEM((TM, D_MODEL), F32)],
        compiler_params=_params("arbitrary", "arbitrary"),
        name="ffn_dense",
    )(x, wg, wu, wd, g.reshape(1, -1), b.reshape(1, -1))


def _router_kernel(x_ref, w_ref, b_ref, sel_ref, prob_ref):
    logits = jnp.dot(x_ref[...], w_ref[...], precision=lax.Precision.HIGHEST,
                     preferred_element_type=F32) + b_ref[...]
    lane = lax.broadcasted_iota(jnp.int32, logits.shape, 1)
    neg = jnp.float32(-jnp.inf)
    l1 = jnp.where(lane < N_EXPERTS, logits, neg)
    m1 = jnp.max(l1, axis=1, keepdims=True)
    i1 = jnp.min(jnp.where(l1 == m1, lane, LANES), axis=1, keepdims=True)
    l2 = jnp.where(lane == i1, neg, l1)
    m2 = jnp.max(l2, axis=1, keepdims=True)
    i2 = jnp.min(jnp.where(l2 == m2, lane, LANES), axis=1, keepdims=True)
    e2 = jnp.exp(m2 - m1)
    den = 1.0 + e2
    sel_ref[...] = jnp.where(lane == 0, i1, jnp.where(lane == 1, i2, 0))
    prob_ref[...] = jnp.where(lane == 0, 1.0 / den, jnp.where(lane == 1, e2 / den, 0.0))


def router(x, w_router, b_router):
    m = x.shape[0]
    wp = jnp.zeros((D_MODEL, LANES), F32).at[:, :N_EXPERTS].set(w_router)
    bp = jnp.zeros((1, LANES), F32).at[0, :N_EXPERTS].set(b_router)
    return pl.pallas_call(
        _router_kernel,
        grid=(m // TM,),
        in_specs=[pl.BlockSpec((TM, D_MODEL), lambda i: (i, 0)),
                  pl.BlockSpec((D_MODEL, LANES), lambda i: (0, 0)),
                  pl.BlockSpec((1, LANES), lambda i: (0, 0))],
        out_specs=[pl.BlockSpec((TM, LANES), lambda i: (i, 0)),
                   pl.BlockSpec((TM, LANES), lambda i: (i, 0))],
        out_shape=[jax.ShapeDtypeStruct((m, LANES), jnp.int32),
                   jax.ShapeDtypeStruct((m, LANES), F32)],
        compiler_params=_params("arbitrary"),
        name="router",
    )(x, wp, bp)


def _route(sel, tm, n_tiles):
    e = sel[:, :TOP_K].reshape(-1)
    onehot = (e[:, None] == jnp.arange(N_EXPERTS, dtype=jnp.int32)[None, :]).astype(jnp.int32)
    csum = jnp.cumsum(onehot, axis=0)
    rank = jnp.sum(onehot * csum, axis=1) - 1
    counts = csum[-1]
    padded = ((counts + tm - 1) // tm) * tm
    ends = jnp.cumsum(padded)
    starts = ends - padded
    dest = jnp.sum(onehot * starts[None, :], axis=1) + rank
    first_row = jnp.arange(n_tiles, dtype=jnp.int32) * tm
    tile_expert = jnp.sum((first_row[:, None] >= ends[None, :]).astype(jnp.int32), axis=1)
    tile_expert = jnp.minimum(tile_expert, N_EXPERTS - 1)
    n_used = (ends[-1] // tm).reshape(1)
    return dest.astype(jnp.int32), tile_expert.astype(jnp.int32), n_used.astype(jnp.int32)


def _token_copy(src_ref, dst_ref, sem, s_tok, d_tok, n_tok):
    s0 = pl.multiple_of(s_tok * TOK, TOK)
    d0 = pl.multiple_of(d_tok * TOK, TOK)
    return pltpu.make_async_copy(src_ref.at[pl.ds(s0, n_tok * TOK), :],
                                 dst_ref.at[pl.ds(d0, n_tok * TOK), :], sem)


def _to_token_tiles(dst_ref, src, rows):
    for s in range(TOK):
        dst_ref[pl.ds(s, rows, stride=TOK), :] = src[:, s * LANES:(s + 1) * LANES]


def _dispatch_kernel(didx_ref, x_ref, init_ref, out_ref, tok_buf, sem):
    del init_ref
    _to_token_tiles(tok_buf, x_ref[...], TM)

    def issue(t, carry):
        for k in range(TOP_K):
            _token_copy(tok_buf, out_ref, sem, t, didx_ref[0, TOP_K * t + k], 1).start()
        return carry

    lax.fori_loop(0, TM, issue, 0)
    for _ in range(TOP_K):
        _token_copy(tok_buf, out_ref, sem, 0, 0, TM).wait()


def dispatch(x, dest, grouped):
    m = x.shape[0]
    nc = m // TM
    return pl.pallas_call(
        _dispatch_kernel,
        grid=(nc,),
        in_specs=[pl.BlockSpec((None, 1, TOP_K * TM), lambda c: (c, 0, 0),
                               memory_space=pltpu.SMEM),
                  pl.BlockSpec((TM, D_MODEL), lambda c: (c, 0)),
                  pl.BlockSpec(memory_space=pl.ANY)],
        out_specs=pl.BlockSpec(memory_space=pl.ANY),
        out_shape=jax.ShapeDtypeStruct(grouped.shape, grouped.dtype),
        scratch_shapes=[pltpu.VMEM((TM * TOK, LANES), F32), pltpu.SemaphoreType.DMA(())],
        input_output_aliases={2: 0},
        compiler_params=_params("arbitrary"),
        name="dispatch",
    )(dest.reshape(nc, 1, TOP_K * TM), x, grouped)


def _gmm_kernel(te_ref, nu_ref, x_ref, wg_ref, wu_ref, wd_ref, y_ref, xb_ref, acc_ref):
    del te_ref
    i = pl.program_id(0)
    j = pl.program_id(1)

    @pl.when((i >= nu_ref[0]) & (j == 0))
    def _():
        y_ref[...] = jnp.zeros_like(y_ref)

    @pl.when(i < nu_ref[0])
    def _():
        @pl.when(j == 0)
        def _():
            for s in range(TOK):
                xb_ref[:, s * LANES:(s + 1) * LANES] = (
                    x_ref[pl.ds(s, TM, stride=TOK), :].astype(BF16))

        xb = xb_ref[...]
        hg = jnp.dot(xb, wg_ref[...].astype(BF16), preferred_element_type=F32)
        hu = jnp.dot(xb, wu_ref[...].astype(BF16), preferred_element_type=F32)
        y = _bdot(jax.nn.silu(hg) * hu, wd_ref[...])

        @pl.when(j == 0)
        def _():
            acc_ref[...] = y

        @pl.when(j > 0)
        def _():
            acc_ref[...] += y

        @pl.when(j == pl.num_programs(1) - 1)
        def _():
            _to_token_tiles(y_ref, acc_ref, TM)


def gmm(xs, tile_expert, n_used, wg, wu, wd, l):
    rows = xs.shape[0] // TOK
    n_tiles = rows // TM
    nf = D_FF // TF

    def tile(i, nu):
        return jnp.minimum(i, nu[0] - 1)

    def ff(i, j, nu):
        return jnp.where(i < nu[0], j, nf - 1)

    grid_spec = pltpu.PrefetchScalarGridSpec(
        num_scalar_prefetch=2,
        grid=(n_tiles, nf),
        in_specs=[pl.BlockSpec((TM * TOK, LANES), lambda i, j, te, nu: (i, 0)),
                  pl.BlockSpec((None, None, D_MODEL, TF),
                               lambda i, j, te, nu: (l, te[tile(i, nu)], 0, ff(i, j, nu))),
                  pl.BlockSpec((None, None, D_MODEL, TF),
                               lambda i, j, te, nu: (l, te[tile(i, nu)], 0, ff(i, j, nu))),
                  pl.BlockSpec((None, None, TF, D_MODEL),
                               lambda i, j, te, nu: (l, te[tile(i, nu)], ff(i, j, nu), 0))],
        out_specs=pl.BlockSpec((TM * TOK, LANES), lambda i, j, te, nu: (i, 0)),
        scratch_shapes=[pltpu.VMEM((TM, D_MODEL), BF16), pltpu.VMEM((TM, D_MODEL), F32)],
    )
    return pl.pallas_call(
        _gmm_kernel,
        grid_spec=grid_spec,
        out_shape=jax.ShapeDtypeStruct(xs.shape, F32),
        compiler_params=_params("arbitrary", "arbitrary"),
        name="gmm",
    )(tile_expert, n_used, xs, wg, wu, wd)


def _combine_kernel(didx_ref, x_ref, prob_ref, g_ref, b_ref, y_ref, o_ref, ybuf, sem):
    def issue(t, carry):
        for k in range(TOP_K):
            _token_copy(y_ref, ybuf, sem, didx_ref[0, TOP_K * t + k], k * TM + t, 1).start()
        return carry

    lax.fori_loop(0, TM, issue, 0)
    for k in range(TOP_K):
        _token_copy(y_ref, ybuf, sem, 0, k * TM, TM).wait()
    p0 = prob_ref[:, 0:1]
    p1 = prob_ref[:, 1:2]
    f = jnp.concatenate(
        [p0 * ybuf[pl.ds(s, TM, stride=TOK), :] + p1 * ybuf[pl.ds(TM * TOK + s, TM, stride=TOK), :]
         for s in range(TOK)], axis=1)
    o_ref[...] = _layer_norm(ALPHA * x_ref[...] + f, g_ref[...], b_ref[...])


def combine(x, y, dest, prob, g, b):
    m = x.shape[0]
    nc = m // TM
    return pl.pallas_call(
        _combine_kernel,
        grid=(nc,),
        in_specs=[pl.BlockSpec((None, 1, TOP_K * TM), lambda i: (i, 0, 0),
                               memory_space=pltpu.SMEM),
                  pl.BlockSpec((TM, D_MODEL), lambda i: (i, 0)),
                  pl.BlockSpec((TM, LANES), lambda i: (i, 0)),
                  pl.BlockSpec((1, D_MODEL), lambda i: (0, 0)),
                  pl.BlockSpec((1, D_MODEL), lambda i: (0, 0)),
                  pl.BlockSpec(memory_space=pl.ANY)],
        out_specs=pl.BlockSpec((TM, D_MODEL), lambda i: (i, 0)),
        out_shape=jax.ShapeDtypeStruct((m, D_MODEL), F32),
        scratch_shapes=[pltpu.VMEM((TOP_K * TM * TOK, LANES), F32),
                        pltpu.SemaphoreType.DMA(())],
        compiler_params=_params("arbitrary"),
        name="combine",
    )(dest.reshape(nc, 1, TOP_K * TM), x, prob, g.reshape(1, -1), b.reshape(1, -1), y)


def moe_routed(xs_list, p, l):
    j = l // 2
    routed = [router(x, p["w_router"][j], p["b_router"][j]) for x in xs_list]
    sel = jnp.concatenate([r[0] for r in routed], axis=0)
    n_pairs = TOP_K * sel.shape[0]
    n_tiles = n_pairs // TM + N_EXPERTS
    dest, tile_expert, n_used = _route(sel, TM, n_tiles)
    bounds = [0]
    for x in xs_list:
        bounds.append(bounds[-1] + TOP_K * x.shape[0])

    grouped = jnp.zeros((n_tiles * TM * TOK, LANES), F32)
    for x, lo, hi in zip(xs_list, bounds[:-1], bounds[1:]):
        grouped = dispatch(x, dest[lo:hi], grouped)
    y = gmm(grouped, tile_expert, n_used, p["w_e_gate"], p["w_e_up"], p["w_e_down"], j)
    return [combine(x, y, dest[lo:hi], prob, p["ln2_g"][l], p["ln2_b"][l])
            for x, (_, prob), lo, hi in zip(xs_list, routed, bounds[:-1], bounds[1:])]


def _layer_weights(l, p):
    row = lambda a: a[l].reshape(1, -1)
    return {
        "w_conv_a": p["w_conv_a"][l], "b_conv_a": row(p["b_conv_a"]),
        "ln_a_g": row(p["ln_a_g"]), "ln_a_b": row(p["ln_a_b"]),
        "w_a_out": p["w_a_out"][l].astype(BF16),
        "w_conv_b": p["w_conv_b"][l], "b_conv_b": row(p["b_conv_b"]),
        "w_rg": jnp.concatenate([p["w_rg_a"][l], p["w_rg_x"][l]], axis=-1).astype(BF16),
        "b_rg_a": row(p["b_rg_a"]), "b_rg_x": row(p["b_rg_x"]), "lam": row(p["lru_lambda"]),
        "w_b_out": p["w_b_out"][l].astype(BF16), "w_c_out": p["w_c_out"][l].astype(BF16),
        "w_mix_out": p["w_mix_out"][l].astype(BF16),
        "ln1_g": row(p["ln1_g"]), "ln1_b": row(p["ln1_b"]),
    }


def _mix_layer(l, p, w, x_tm, sa_tm, sb_tm, h0, attend, tt, nb, q_from_proj):
    t, nbt, _ = x_tm.shape
    m = t * nbt
    b_in = p["b_in"].reshape(DEPTH, 1, D_IN)
    if q_from_proj:
        z, q = matmul_bias(x_tm.reshape(m, D_MODEL), p["w_in"], b_in, l, Z_BLOCK, "in_proj",
                           nb_seq=nbt)
        z3 = z.reshape(t, nbt, D_IN)
        o = attend(q)
    else:
        z, = matmul_bias(x_tm.reshape(m, D_MODEL), p["w_in"], b_in, l, Z_BLOCK, "in_proj")
        z3 = z.reshape(t, nbt, D_IN)
        q = jnp.transpose(z3[:, :, Q_BLOCK * Z_BLOCK:(Q_BLOCK + 1) * Z_BLOCK], (1, 0, 2))
        o = jnp.transpose(attend(q), (1, 0, 2))
    x1, nsa, nsb, hl = mix(z3, o, x_tm, sa_tm, sb_tm, h0, w, tt, nb, o_bm=q_from_proj)
    return x1.reshape(m, D_MODEL), nsa, nsb, hl


def _to_tm(a):
    return jnp.transpose(a, (1, 0, 2))


def kernel(x_prompt, x_sample, state_conv_a, state_conv_b, state_rglru, cache_mem_k, cache_mem_v, mem_prompt, w_in, b_in, w_conv_a, b_conv_a, ln_a_g, ln_a_b, w_a_out, w_conv_b, b_conv_b, w_rg_a, b_rg_a, w_rg_x, b_rg_x, lru_lambda, w_b_out, w_mem_kv, w_c_out, w_mix_out, ln1_g, ln1_b, w_ff_gate, w_ff_up, w_ff_down, w_router, b_router, w_e_gate, w_e_up, w_e_down, ln2_g, ln2_b):
    p = dict(w_in=w_in, b_in=b_in, w_conv_a=w_conv_a, b_conv_a=b_conv_a, ln_a_g=ln_a_g,
             ln_a_b=ln_a_b, w_a_out=w_a_out, w_conv_b=w_conv_b, b_conv_b=b_conv_b,
             w_rg_a=w_rg_a, b_rg_a=b_rg_a, w_rg_x=w_rg_x, b_rg_x=b_rg_x, lru_lambda=lru_lambda,
             w_b_out=w_b_out, w_c_out=w_c_out, w_mix_out=w_mix_out, ln1_g=ln1_g, ln1_b=ln1_b,
             w_ff_gate=w_ff_gate, w_ff_up=w_ff_up, w_ff_down=w_ff_down, w_router=w_router,
             b_router=b_router, w_e_gate=w_e_gate, w_e_up=w_e_up, w_e_down=w_e_down,
             ln2_g=ln2_g, ln2_b=ln2_b)
    bp, tp, _ = x_prompt.shape
    bs, ts, _ = x_sample.shape
    n_mem = mem_prompt.shape[1]

    yp = _to_tm(x_prompt)
    ys = _to_tm(x_sample)
    zero_a = jnp.zeros((K_A - 1, bp, D_A), F32)
    zero_b = jnp.zeros((K_B - 1, bp, D_B), F32)
    zero_h = jnp.zeros((bp, D_B), F32)
    zero_bias = jnp.zeros((DEPTH, 1, 2 * D_C), F32)
    cache_k = cache_mem_k.reshape(DEPTH, bs, n_mem * H_C, DH_C)
    cache_v = cache_mem_v.reshape(DEPTH, bs, n_mem * H_C, DH_C)
    outs = {k: [] for k in ("pa", "pb", "ph", "pk", "pv", "sa", "sb", "sh")}
    for l in range(DEPTH):
        w = _layer_weights(l, p)
        kv, = matmul_bias(mem_prompt.reshape(bp * n_mem, D_MODEL), w_mem_kv, zero_bias, l,
                          Z_BLOCK, "kv_proj")
        kv = kv.reshape(bp, n_mem, 2 * D_C)
        mk = kv[..., :D_C]
        mv = kv[..., D_C:]
        kv4 = kv.reshape(1, bp, n_mem, 2 * D_C)
        xp1, na, nb_, nh = _mix_layer(
            l, p, w, yp, zero_a, zero_b, zero_h,
            lambda q: attention(q, kv4, kv4, (0, 0, 1), sb=1, tq=512),
            tt=32, nb=bp, q_from_proj=True)
        outs["pa"].append(_to_tm(na))
        outs["pb"].append(_to_tm(nb_))
        outs["ph"].append(nh)
        outs["pk"].append(mk.reshape(bp, n_mem, H_C, DH_C))
        outs["pv"].append(mv.reshape(bp, n_mem, H_C, DH_C))
        xs1, na, nb_, nh = _mix_layer(
            l, p, w, ys, _to_tm(state_conv_a[l]), _to_tm(state_conv_b[l]), state_rglru[l],
            lambda q, l=l: attention_rows(q, cache_k, cache_v, l, sb=8),
            tt=ts, nb=32, q_from_proj=False)
        outs["sa"].append(_to_tm(na))
        outs["sb"].append(_to_tm(nb_))
        outs["sh"].append(nh)
        if l % 2 == 0:
            j = l // 2
            xp2, xs2 = [ffn_dense(x, w_ff_gate, w_ff_up, w_ff_down, ln2_g[l], ln2_b[l], j)
                        for x in (xp1, xs1)]
        else:
            xp2, xs2 = moe_routed([xp1, xs1], p, l)
        yp = xp2.reshape(tp, bp, D_MODEL)
        ys = xs2.reshape(ts, bs, D_MODEL)

    st = lambda k: jnp.stack(outs[k])
    return (_to_tm(yp), _to_tm(ys), st("pa"), st("pb"), st("ph"), st("pk"), st("pv"),
            st("sa"), st("sb"), st("sh"))
```

```python
import functools

import jax
import jax.numpy as jnp
from jax import lax
from jax.experimental import pallas as pl
from jax.experimental.pallas import tpu as pltpu

D_MODEL = 1024
N_MEM = 256
D_A = 512
K_A = 31
D_B = 1024
K_B = 4
H_B = 8
BW_B = D_B // H_B
LRU_C = 8.0
D_C = 512
H_C = 4
DH_C = D_C // H_C
N_BRANCH = 3
D_IN = 2 * D_A + 2 * D_B + D_C + N_BRANCH * D_MODEL
D_FF = 2816
N_EXPERTS = 8
TOP_K = 2
DEPTH = 2
ALPHA = (2.0 * DEPTH) ** 0.25
LN_EPS = 1e-5

Z_BLOCK = 512
Q_BLOCK = (2 * D_A + 2 * D_B) // Z_BLOCK
ZM_COL = 2 * D_A + 2 * D_B + D_C

LANES = 128
VMEM_LIMIT = 56 * 1024 * 1024

TM = 1024
TF = 256
TOK = D_MODEL // LANES
assert TOK == 8

BF16 = jnp.bfloat16
F32 = jnp.float32


def _params(*sem):
    return pltpu.CompilerParams(dimension_semantics=sem, vmem_limit_bytes=VMEM_LIMIT)


def _layer_norm(x, g, b):
    mu = jnp.mean(x, axis=-1, keepdims=True)
    xc = x - mu
    var = jnp.mean(xc * xc, axis=-1, keepdims=True)
    return xc * lax.rsqrt(var + LN_EPS) * g + b


def _bdot(a, b):
    return jnp.dot(a.astype(BF16), b.astype(BF16), preferred_element_type=F32)


def _matmul_bias_kernel(x_ref, w_ref, b_ref, o_ref, xb_ref):
    @pl.when(pl.program_id(1) == 0)
    def _():
        xb_ref[...] = x_ref[...].astype(BF16)

    o_ref[...] = jnp.dot(xb_ref[...], w_ref[...].astype(BF16),
                         preferred_element_type=F32) + b_ref[...]


def matmul_bias(x, w, b, l, tn, name):
    m, k = x.shape
    n = w.shape[2]
    return pl.pallas_call(
        _matmul_bias_kernel,
        grid=(m // TM, n // tn),
        in_specs=[pl.BlockSpec((TM, k), lambda i, j: (i, 0)),
                  pl.BlockSpec((None, k, tn), lambda i, j: (l, 0, j)),
                  pl.BlockSpec((None, 1, tn), lambda i, j: (l, 0, j))],
        out_specs=pl.BlockSpec((TM, tn), lambda i, j: (i, j)),
        out_shape=jax.ShapeDtypeStruct((m, n), F32),
        scratch_shapes=[pltpu.VMEM((TM, k), BF16)],
        compiler_params=_params("arbitrary", "arbitrary"),
        name=name,
    )(x, w, b)


def _attention_kernel(x_ref, wq_ref, bq_ref, k_ref, v_ref, o_ref):
    q = jnp.dot(x_ref[...].astype(BF16), wq_ref[...], preferred_element_type=F32) + bq_ref[...]
    for h in range(H_C):
        cols = slice(h * DH_C, (h + 1) * DH_C)
        kh = k_ref[:, cols].astype(BF16)
        vh = v_ref[:, cols].astype(BF16)
        sc = lax.dot_general(q[:, cols].astype(BF16), kh, (((1,), (1,)), ((), ())),
                             preferred_element_type=F32) * (DH_C ** -0.5)
        e = jnp.exp(sc - jnp.max(sc, axis=-1, keepdims=True))
        p = e * (1.0 / jnp.sum(e, axis=-1, keepdims=True))
        o_ref[:, cols] = jnp.dot(p.astype(BF16), vh, preferred_element_type=F32)


def attention(x, w_in_bf, b_in, l, k, v, kv_index, tq):
    s, t, _ = x.shape
    lk, kc, vc = kv_index
    return pl.pallas_call(
        _attention_kernel,
        grid=(s, t // tq),
        in_specs=[pl.BlockSpec((None, tq, D_MODEL), lambda i, j: (i, j, 0)),
                  pl.BlockSpec((None, D_MODEL, D_C), lambda i, j: (l, 0, Q_BLOCK)),
                  pl.BlockSpec((None, 1, D_C), lambda i, j: (l, 0, Q_BLOCK)),
                  pl.BlockSpec((None, None, N_MEM, D_C), lambda i, j: (lk, i, 0, kc)),
                  pl.BlockSpec((None, None, N_MEM, D_C), lambda i, j: (lk, i, 0, vc))],
        out_specs=pl.BlockSpec((None, tq, D_C), lambda i, j: (i, j, 0)),
        out_shape=jax.ShapeDtypeStruct((s, t, D_C), F32),
        compiler_params=_params("arbitrary", "arbitrary"),
        name="attention",
    )(x, w_in_bf, b_in, k, v)


def _attention_rows_kernel(x_ref, wq_ref, bq_ref, k_ref, v_ref, o_ref, *, sb, t):
    n_rows = H_C * t
    n_cols = N_MEM * H_C
    row_head = lax.broadcasted_iota(jnp.int32, (n_rows, n_cols), 0) // t
    col_head = lax.broadcasted_iota(jnp.int32, (n_rows, n_cols), 1) % H_C
    own = row_head == col_head
    qs = (jnp.dot(x_ref[...].reshape(sb * t, D_MODEL).astype(BF16), wq_ref[...],
                  preferred_element_type=F32) + bq_ref[...])
    for s in range(sb):
        q = qs[s * t:(s + 1) * t, :]
        qa = jnp.concatenate([q[:, h * DH_C:(h + 1) * DH_C] for h in range(H_C)], axis=0)
        sc = lax.dot_general(qa.astype(BF16), k_ref[s].astype(BF16), (((1,), (1,)), ((), ())),
                             preferred_element_type=F32) * (DH_C ** -0.5)
        sc = jnp.where(own, sc, -jnp.inf)
        e = jnp.exp(sc - jnp.max(sc, axis=-1, keepdims=True))
        p = e * (1.0 / jnp.sum(e, axis=-1, keepdims=True))
        oa = jnp.dot(p.astype(BF16), v_ref[s].astype(BF16), preferred_element_type=F32)
        o_ref[s] = jnp.concatenate([oa[h * t:(h + 1) * t, :] for h in range(H_C)], axis=1)


def attention_rows(x, w_in_bf, b_in, l, k, v, sb):
    s, t, _ = x.shape
    rows = N_MEM * H_C
    return pl.pallas_call(
        functools.partial(_attention_rows_kernel, sb=sb, t=t),
        grid=(s // sb,),
        in_specs=[pl.BlockSpec((sb, t, D_MODEL), lambda i: (i, 0, 0)),
                  pl.BlockSpec((None, D_MODEL, D_C), lambda i: (l, 0, Q_BLOCK)),
                  pl.BlockSpec((None, 1, D_C), lambda i: (l, 0, Q_BLOCK)),
                  pl.BlockSpec((None, sb, rows, DH_C), lambda i: (l, i, 0, 0)),
                  pl.BlockSpec((None, sb, rows, DH_C), lambda i: (l, i, 0, 0))],
        out_specs=pl.BlockSpec((sb, t, D_C), lambda i: (i, 0, 0)),
        out_shape=jax.ShapeDtypeStruct((s, t, D_C), F32),
        compiler_params=_params("arbitrary"),
        name="attention_rows",
    )(x, w_in_bf, b_in, k, v)


def _mix_kernel(o_ref, x_ref, sa_ref, sb_ref, h0_ref, bin_ref,
                wca_ref, bca_ref, lag_ref, lab_ref, wao_ref,
                wcb_ref, bcb_ref, wrg_ref, bra_ref, brx_ref, lam_ref, wbo_ref,
                wco_ref, wmo_ref, l1g_ref, l1b_ref, win_hbm,
                x1_ref, nsa_ref, nsb_ref, hl_ref,
                ua_buf, xb_buf, a_buf, u_buf, hs_buf, h_carry, o_buf, win_buf, win_sem,
                *, l, tt, nb, nt, o_bm):
    r = tt * nb
    ha = (K_A - 1) * nb
    hb = (K_B - 1) * nb
    i = pl.program_id(1)

    @pl.when((pl.program_id(0) == 0) & (i == 0))
    def _():
        load = pltpu.make_async_copy(win_hbm.at[l], win_buf, win_sem)
        load.start()
        load.wait()

    @pl.when(i == 0)
    def _():
        ua_buf[0:ha, :] = sa_ref[...].reshape(ha, D_A)
        xb_buf[0:hb, :] = sb_ref[...].reshape(hb, D_B)
        h_carry[...] = h0_ref[...]

    x = x_ref[...].reshape(r, D_MODEL)
    xb = x.astype(BF16)

    def zcols(lo, hi):
        return (jnp.dot(xb, win_buf[:, lo:hi], preferred_element_type=F32) + bin_ref[:, lo:hi])

    za = zcols(0, 2 * D_A)
    ua_buf[ha:ha + r, :] = za[:, :D_A] * jax.nn.sigmoid(za[:, D_A:])
    c_a = jnp.broadcast_to(bca_ref[...], (r, D_A))
    for k in range(K_A):
        c_a = c_a + wca_ref[k:k + 1, :] * ua_buf[k * nb:k * nb + r, :]
    y_a = _bdot(jax.nn.silu(_layer_norm(c_a, lag_ref[...], lab_ref[...])), wao_ref[...])
    nsa_ref[...] = ua_buf[r:r + ha, :].reshape(K_A - 1, nb, D_A)

    xb_buf[hb:hb + r, :] = zcols(2 * D_A, 2 * D_A + D_B)
    c_b = jnp.broadcast_to(bcb_ref[...], (r, D_B))
    for k in range(K_B):
        c_b = c_b + wcb_ref[k:k + 1, :] * xb_buf[k * nb:k * nb + r, :]
    nsb_ref[...] = xb_buf[r:r + hb, :].reshape(K_B - 1, nb, D_B)
    c_bf = c_b.astype(BF16)
    ra, rx = [], []
    for h in range(H_B):
        gh = jnp.dot(c_bf[:, h * BW_B:(h + 1) * BW_B], wrg_ref[h], preferred_element_type=F32)
        ra.append(gh[:, :BW_B])
        rx.append(gh[:, BW_B:])
    gate_r = jax.nn.sigmoid(jnp.concatenate(ra, axis=1) + bra_ref[...])
    gate_i = jax.nn.sigmoid(jnp.concatenate(rx, axis=1) + brx_ref[...])
    log_a = (-LRU_C) * gate_r * jax.nn.softplus(-lam_ref[...])
    a_buf[...] = jnp.exp(log_a)
    th = jnp.tanh(log_a)
    u_buf[...] = jnp.sqrt(-2.0 * th / (1.0 - th)) * (gate_i * c_b)
    h = h_carry[...]
    for t in range(tt):
        rows = slice(t * nb, (t + 1) * nb)
        h = a_buf[rows, :] * h + u_buf[rows, :]
        hs_buf[rows, :] = h
    h_carry[...] = h
    hl_ref[...] = h
    z_g = zcols(2 * D_A + D_B, 2 * D_A + 2 * D_B)
    y_b = _bdot(hs_buf[...] * jax.nn.gelu(z_g), wbo_ref[...])

    if nt > 1:
        ua_buf[0:ha, :] = ua_buf[r:r + ha, :]
        xb_buf[0:hb, :] = xb_buf[r:r + hb, :]

    if o_bm:
        for s in range(nb):
            for c in range(D_C // LANES):
                o_buf[c, pl.ds(s, tt, stride=nb), :] = o_ref[s, :, c * LANES:(c + 1) * LANES]
        o_tm = jnp.concatenate([o_buf[c] for c in range(D_C // LANES)], axis=1)
    else:
        o_tm = o_ref[...].reshape(r, D_C)
    y_c = _bdot(o_tm, wco_ref[...])

    def gate(branch):
        lo = ZM_COL + branch * D_MODEL
        return jax.nn.sigmoid(zcols(lo, lo + D_MODEL))

    merged = gate(0) * y_a
    merged = merged + gate(1) * y_b
    merged = merged + gate(2) * y_c
    mix = _bdot(merged, wmo_ref[...])
    x1_ref[...] = _layer_norm(ALPHA * x + mix, l1g_ref[...], l1b_ref[...]).reshape(tt, nb, D_MODEL)


def mix(l, o, x_tm, sa_tm, sb_tm, h0, w, w_in_bf, tt, nb, o_bm):
    t, nbt, _ = x_tm.shape
    nt = t // tt
    assert nt == 1 or tt >= K_A - 1
    assert not o_bm or nb == nbt
    r = tt * nb
    grid = (nbt // nb, nt)
    o_spec = (pl.BlockSpec((nb, tt, D_C), lambda g, i: (0, i, 0)) if o_bm
              else pl.BlockSpec((tt, nb, D_C), lambda g, i: (i, g, 0)))

    def zspec(width, blk):
        return pl.BlockSpec((tt, nb, width), lambda g, i: (i, g, blk))

    def full(a):
        nd = a.ndim
        return pl.BlockSpec(a.shape, lambda g, i: (0,) * nd)

    weights = [w["b_in"], w["w_conv_a"], w["b_conv_a"], w["ln_a_g"], w["ln_a_b"], w["w_a_out"],
               w["w_conv_b"], w["b_conv_b"], w["w_rg"], w["b_rg_a"], w["b_rg_x"], w["lam"],
               w["w_b_out"], w["w_c_out"], w["w_mix_out"], w["ln1_g"], w["ln1_b"]]
    in_specs = ([o_spec, zspec(D_MODEL, 0),
                 pl.BlockSpec((K_A - 1, nb, D_A), lambda g, i: (0, g, 0)),
                 pl.BlockSpec((K_B - 1, nb, D_B), lambda g, i: (0, g, 0)),
                 pl.BlockSpec((nb, D_B), lambda g, i: (g, 0))]
                + [full(a) for a in weights]
                + [pl.BlockSpec(memory_space=pl.ANY)])
    out_specs = [zspec(D_MODEL, 0),
                 pl.BlockSpec((K_A - 1, nb, D_A), lambda g, i: (0, g, 0)),
                 pl.BlockSpec((K_B - 1, nb, D_B), lambda g, i: (0, g, 0)),
                 pl.BlockSpec((nb, D_B), lambda g, i: (g, 0))]
    out_shape = [jax.ShapeDtypeStruct((t, nbt, D_MODEL), F32),
                 jax.ShapeDtypeStruct((K_A - 1, nbt, D_A), F32),
                 jax.ShapeDtypeStruct((K_B - 1, nbt, D_B), F32),
                 jax.ShapeDtypeStruct((nbt, D_B), F32)]
    scratch = [pltpu.VMEM(((K_A - 1) * nb + r, D_A), F32),
               pltpu.VMEM(((K_B - 1) * nb + r, D_B), F32),
               pltpu.VMEM((r, D_B), F32), pltpu.VMEM((r, D_B), F32), pltpu.VMEM((r, D_B), F32),
               pltpu.VMEM((nb, D_B), F32), pltpu.VMEM((D_C // LANES, r, LANES), F32),
               pltpu.VMEM((D_MODEL, D_IN), BF16), pltpu.SemaphoreType.DMA(())]
    return pl.pallas_call(
        functools.partial(_mix_kernel, l=l, tt=tt, nb=nb, nt=nt, o_bm=o_bm),
        grid=grid, in_specs=in_specs, out_specs=out_specs, out_shape=out_shape,
        scratch_shapes=scratch,
        compiler_params=_params("arbitrary", "arbitrary"),
        name="mix",
    )(o, x_tm, sa_tm, sb_tm, h0, *weights, w_in_bf)


def _ffn_kernel(x_ref, wg_ref, wu_ref, wd_ref, g_ref, b_ref, o_ref, xb_ref, acc_ref):
    j = pl.program_id(1)

    @pl.when(j == 0)
    def _():
        xb_ref[...] = x_ref[...].astype(BF16)
        acc_ref[...] = jnp.zeros_like(acc_ref)

    xb = xb_ref[...]
    hg = jnp.dot(xb, wg_ref[...].astype(BF16), preferred_element_type=F32)
    hu = jnp.dot(xb, wu_ref[...].astype(BF16), preferred_element_type=F32)
    acc_ref[...] += _bdot(jax.nn.silu(hg) * hu, wd_ref[...])

    @pl.when(j == pl.num_programs(1) - 1)
    def _():
        o_ref[...] = _layer_norm(ALPHA * x_ref[...] + acc_ref[...], g_ref[...], b_ref[...])


def ffn_dense(x, wg, wu, wd, g, b, l):
    m = x.shape[0]
    return pl.pallas_call(
        _ffn_kernel,
        grid=(m // TM, D_FF // TF),
        in_specs=[pl.BlockSpec((TM, D_MODEL), lambda i, j: (i, 0)),
                  pl.BlockSpec((None, D_MODEL, TF), lambda i, j: (l, 0, j)),
                  pl.BlockSpec((None, D_MODEL, TF), lambda i, j: (l, 0, j)),
                  pl.BlockSpec((None, TF, D_MODEL), lambda i, j: (l, j, 0)),
                  pl.BlockSpec((1, D_MODEL), lambda i, j: (0, 0)),
                  pl.BlockSpec((1, D_MODEL), lambda i, j: (0, 0))],
        out_specs=pl.BlockSpec((TM, D_MODEL), lambda i, j: (i, 0)),
        out_shape=jax.ShapeDtypeStruct((m, D_MODEL), F32),
        scratch_shapes=[pltpu.VMEM((TM, D_MODEL), BF16), pltpu.VMEM((TM, D_MODEL), F32)],
        compiler_params=_params("arbitrary", "arbitrary"),
        name="ffn_dense",
    )(x, wg, wu, wd, g.reshape(1, -1), b.reshape(1, -1))


def _router_kernel(x_ref, w_ref, b_ref, sel_ref, prob_ref):
    logits = jnp.dot(x_ref[...], w_ref[...], precision=lax.Precision.HIGHEST,
                     preferred_element_type=F32) + b_ref[...]
    lane = lax.broadcasted_iota(jnp.int32, logits.shape, 1)
    neg = jnp.float32(-jnp.inf)
    l1 = jnp.where(lane < N_EXPERTS, logits, neg)
    m1 = jnp.max(l1, axis=1, keepdims=True)
    i1 = jnp.min(jnp.where(l1 == m1, lane, LANES), axis=1, keepdims=True)
    l2 = jnp.where(lane == i1, neg, l1)
    m2 = jnp.max(l2, axis=1, keepdims=True)
    i2 = jnp.min(jnp.where(l2 == m2, lane, LANES), axis=1, keepdims=True)
    e2 = jnp.exp(m2 - m1)
    den = 1.0 + e2
    sel_ref[...] = jnp.where(lane == 0, i1, jnp.where(lane == 1, i2, 0))
    prob_ref[...] = jnp.where(lane == 0, 1.0 / den, jnp.where(lane == 1, e2 / den, 0.0))


def router(x, w_router, b_router):
    m = x.shape[0]
    wp = jnp.zeros((D_MODEL, LANES), F32).at[:, :N_EXPERTS].set(w_router)
    bp = jnp.zeros((1, LANES), F32).at[0, :N_EXPERTS].set(b_router)
    return pl.pallas_call(
        _router_kernel,
        grid=(m // TM,),
        in_specs=[pl.BlockSpec((TM, D_MODEL), lambda i: (i, 0)),
                  pl.BlockSpec((D_MODEL, LANES), lambda i: (0, 0)),
                  pl.BlockSpec((1, LANES), lambda i: (0, 0))],
        out_specs=[pl.BlockSpec((TM, LANES), lambda i: (i, 0)),
                   pl.BlockSpec((TM, LANES), lambda i: (i, 0))],
        out_shape=[jax.ShapeDtypeStruct((m, LANES), jnp.int32),
                   jax.ShapeDtypeStruct((m, LANES), F32)],
        compiler_params=_params("arbitrary"),
        name="router",
    )(x, wp, bp)


def _route(sel, tm, n_tiles):
    e = sel[:, :TOP_K].reshape(-1)
    onehot = (e[:, None] == jnp.arange(N_EXPERTS, dtype=jnp.int32)[None, :]).astype(jnp.int32)
    csum = jnp.cumsum(onehot, axis=0)
    rank = jnp.sum(onehot * csum, axis=1) - 1
    counts = csum[-1]
    padded = ((counts + tm - 1) // tm) * tm
    ends = jnp.cumsum(padded)
    starts = ends - padded
    dest = jnp.sum(onehot * starts[None, :], axis=1) + rank
    first_row = jnp.arange(n_tiles, dtype=jnp.int32) * tm
    tile_expert = jnp.sum((first_row[:, None] >= ends[None, :]).astype(jnp.int32), axis=1)
    tile_expert = jnp.minimum(tile_expert, N_EXPERTS - 1)
    n_used = (ends[-1] // tm).reshape(1)
    return dest.astype(jnp.int32), tile_expert.astype(jnp.int32), n_used.astype(jnp.int32)


def _token_copy(src_ref, dst_ref, sem, s_tok, d_tok, n_tok):
    s0 = pl.multiple_of(s_tok * TOK, TOK)
    d0 = pl.multiple_of(d_tok * TOK, TOK)
    return pltpu.make_async_copy(src_ref.at[pl.ds(s0, n_tok * TOK), :],
                                 dst_ref.at[pl.ds(d0, n_tok * TOK), :], sem)


def _to_token_tiles(dst_ref, src, rows):
    for s in range(TOK):
        dst_ref[pl.ds(s, rows, stride=TOK), :] = src[:, s * LANES:(s + 1) * LANES]


def _dispatch_kernel(didx_ref, x_ref, init_ref, out_ref, tok_buf, sem):
    del init_ref
    _to_token_tiles(tok_buf, x_ref[...], TM)

    def issue(t, carry):
        for k in range(TOP_K):
            _token_copy(tok_buf, out_ref, sem, t, didx_ref[0, TOP_K * t + k], 1).start()
        return carry

    lax.fori_loop(0, TM, issue, 0)
    for _ in range(TOP_K):
        _token_copy(tok_buf, out_ref, sem, 0, 0, TM).wait()


def dispatch(x, dest, grouped):
    m = x.shape[0]
    nc = m // TM
    return pl.pallas_call(
        _dispatch_kernel,
        grid=(nc,),
        in_specs=[pl.BlockSpec((None, 1, TOP_K * TM), lambda c: (c, 0, 0),
                               memory_space=pltpu.SMEM),
                  pl.BlockSpec((TM, D_MODEL), lambda c: (c, 0)),
                  pl.BlockSpec(memory_space=pl.ANY)],
        out_specs=pl.BlockSpec(memory_space=pl.ANY),
        out_shape=jax.ShapeDtypeStruct(grouped.shape, grouped.dtype),
        scratch_shapes=[pltpu.VMEM((TM * TOK, LANES), F32), pltpu.SemaphoreType.DMA(())],
        input_output_aliases={2: 0},
        compiler_params=_params("arbitrary"),
        name="dispatch",
    )(dest.reshape(nc, 1, TOP_K * TM), x, grouped)


def _gmm_kernel(te_ref, nu_ref, x_ref, wg_ref, wu_ref, wd_ref, y_ref, xb_ref, acc_ref):
    del te_ref
    i = pl.program_id(0)
    j = pl.program_id(1)

    @pl.when((i >= nu_ref[0]) & (j == 0))
    def _():
        y_ref[...] = jnp.zeros_like(y_ref)

    @pl.when(i < nu_ref[0])
    def _():
        @pl.when(j == 0)
        def _():
            for s in range(TOK):
                xb_ref[:, s * LANES:(s + 1) * LANES] = (
                    x_ref[pl.ds(s, TM, stride=TOK), :].astype(BF16))

        xb = xb_ref[...]
        hg = jnp.dot(xb, wg_ref[...].astype(BF16), preferred_element_type=F32)
        hu = jnp.dot(xb, wu_ref[...].astype(BF16), preferred_element_type=F32)
        y = _bdot(jax.nn.silu(hg) * hu, wd_ref[...])

        @pl.when(j == 0)
        def _():
            acc_ref[...] = y

        @pl.when(j > 0)
        def _():
            acc_ref[...] += y

        @pl.when(j == pl.num_programs(1) - 1)
        def _():
            _to_token_tiles(y_ref, acc_ref, TM)


def gmm(xs, tile_expert, n_used, wg, wu, wd, l):
    rows = xs.shape[0] // TOK
    n_tiles = rows // TM
    nf = D_FF // TF

    def tile(i, nu):
        return jnp.minimum(i, nu[0] - 1)

    def ff(i, j, nu):
        return jnp.where(i < nu[0], j, nf - 1)

    grid_spec = pltpu.PrefetchScalarGridSpec(
        num_scalar_prefetch=2,
        grid=(n_tiles, nf),
        in_specs=[pl.BlockSpec((TM * TOK, LANES), lambda i, j, te, nu: (i, 0)),
                  pl.BlockSpec((None, None, D_MODEL, TF),
                               lambda i, j, te, nu: (l, te[tile(i, nu)], 0, ff(i, j, nu))),
                  pl.BlockSpec((None, None, D_MODEL, TF),
                               lambda i, j, te, nu: (l, te[tile(i, nu)], 0, ff(i, j, nu))),
                  pl.BlockSpec((None, None, TF, D_MODEL),
                               lambda i, j, te, nu: (l, te[tile(i, nu)], ff(i, j, nu), 0))],
        out_specs=pl.BlockSpec((TM * TOK, LANES), lambda i, j, te, nu: (i, 0)),
        scratch_shapes=[pltpu.VMEM((TM, D_MODEL), BF16), pltpu.VMEM((TM, D_MODEL), F32)],
    )
    return pl.pallas_call(
        _gmm_kernel,
        grid_spec=grid_spec,
        out_shape=jax.ShapeDtypeStruct(xs.shape, F32),
        compiler_params=_params("arbitrary", "arbitrary"),
        name="gmm",
    )(tile_expert, n_used, xs, wg, wu, wd)


def _combine_kernel(didx_ref, x_ref, prob_ref, g_ref, b_ref, y_ref, o_ref, ybuf, sem):
    def issue(t, carry):
        for k in range(TOP_K):
            _token_copy(y_ref, ybuf, sem, didx_ref[0, TOP_K * t + k], k * TM + t, 1).start()
        return carry

    lax.fori_loop(0, TM, issue, 0)
    for k in range(TOP_K):
        _token_copy(y_ref, ybuf, sem, 0, k * TM, TM).wait()
    p0 = prob_ref[:, 0:1]
    p1 = prob_ref[:, 1:2]
    f = jnp.concatenate(
        [p0 * ybuf[pl.ds(s, TM, stride=TOK), :] + p1 * ybuf[pl.ds(TM * TOK + s, TM, stride=TOK), :]
         for s in range(TOK)], axis=1)
    o_ref[...] = _layer_norm(ALPHA * x_ref[...] + f, g_ref[...], b_ref[...])


def combine(x, y, dest, prob, g, b):
    m = x.shape[0]
    nc = m // TM
    return pl.pallas_call(
        _combine_kernel,
        grid=(nc,),
        in_specs=[pl.BlockSpec((None, 1, TOP_K * TM), lambda i: (i, 0, 0),
                               memory_space=pltpu.SMEM),
                  pl.BlockSpec((TM, D_MODEL), lambda i: (i, 0)),
                  pl.BlockSpec((TM, LANES), lambda i: (i, 0)),
                  pl.BlockSpec((1, D_MODEL), lambda i: (0, 0)),
                  pl.BlockSpec((1, D_MODEL), lambda i: (0, 0)),
                  pl.BlockSpec(memory_space=pl.ANY)],
        out_specs=pl.BlockSpec((TM, D_MODEL), lambda i: (i, 0)),
        out_shape=jax.ShapeDtypeStruct((m, D_MODEL), F32),
        scratch_shapes=[pltpu.VMEM((TOP_K * TM * TOK, LANES), F32),
                        pltpu.SemaphoreType.DMA(())],
        compiler_params=_params("arbitrary"),
        name="combine",
    )(dest.reshape(nc, 1, TOP_K * TM), x, prob, g.reshape(1, -1), b.reshape(1, -1), y)


def moe_routed(xs_list, p, l):
    j = l // 2
    routed = [router(x, p["w_router"][j], p["b_router"][j]) for x in xs_list]
    sel = jnp.concatenate([r[0] for r in routed], axis=0)
    n_pairs = TOP_K * sel.shape[0]
    n_tiles = n_pairs // TM + N_EXPERTS
    dest, tile_expert, n_used = _route(sel, TM, n_tiles)
    bounds = [0]
    for x in xs_list:
        bounds.append(bounds[-1] + TOP_K * x.shape[0])

    grouped = jnp.zeros((n_tiles * TM * TOK, LANES), F32)
    for x, lo, hi in zip(xs_list, bounds[:-1], bounds[1:]):
        grouped = dispatch(x, dest[lo:hi], grouped)
    y = gmm(grouped, tile_expert, n_used, p["w_e_gate"], p["w_e_up"], p["w_e_down"], j)
    return [combine(x, y, dest[lo:hi], prob, p["ln2_g"][l], p["ln2_b"][l])
            for x, (_, prob), lo, hi in zip(xs_list, routed, bounds[:-1], bounds[1:])]


def _layer_weights(l, p):
    row = lambda a: a[l].reshape(1, -1)
    return {
        "b_in": row(p["b_in"]),
        "w_conv_a": p["w_conv_a"][l], "b_conv_a": row(p["b_conv_a"]),
        "ln_a_g": row(p["ln_a_g"]), "ln_a_b": row(p["ln_a_b"]),
        "w_a_out": p["w_a_out"][l].astype(BF16),
        "w_conv_b": p["w_conv_b"][l], "b_conv_b": row(p["b_conv_b"]),
        "w_rg": jnp.concatenate([p["w_rg_a"][l], p["w_rg_x"][l]], axis=-1).astype(BF16),
        "b_rg_a": row(p["b_rg_a"]), "b_rg_x": row(p["b_rg_x"]), "lam": row(p["lru_lambda"]),
        "w_b_out": p["w_b_out"][l].astype(BF16), "w_c_out": p["w_c_out"][l].astype(BF16),
        "w_mix_out": p["w_mix_out"][l].astype(BF16),
        "ln1_g": row(p["ln1_g"]), "ln1_b": row(p["ln1_b"]),
    }


def _mix_layer(l, w, w_in_bf, x_tm, o, sa_tm, sb_tm, h0, tt, nb, o_bm):
    t, nbt, _ = x_tm.shape
    x1, nsa, nsb, hl = mix(l, o, x_tm, sa_tm, sb_tm, h0, w, w_in_bf, tt, nb, o_bm)
    return x1.reshape(t * nbt, D_MODEL), nsa, nsb, hl


def _to_tm(a):
    return jnp.transpose(a, (1, 0, 2))


def kernel(x_prompt, x_sample, state_conv_a, state_conv_b, state_rglru, cache_mem_k, cache_mem_v, mem_prompt, w_in, b_in, w_conv_a, b_conv_a, ln_a_g, ln_a_b, w_a_out, w_conv_b, b_conv_b, w_rg_a, b_rg_a, w_rg_x, b_rg_x, lru_lambda, w_b_out, w_mem_kv, w_c_out, w_mix_out, ln1_g, ln1_b, w_ff_gate, w_ff_up, w_ff_down, w_router, b_router, w_e_gate, w_e_up, w_e_down, ln2_g, ln2_b):
    p = dict(w_in=w_in, b_in=b_in, w_conv_a=w_conv_a, b_conv_a=b_conv_a, ln_a_g=ln_a_g,
             ln_a_b=ln_a_b, w_a_out=w_a_out, w_conv_b=w_conv_b, b_conv_b=b_conv_b,
             w_rg_a=w_rg_a, b_rg_a=b_rg_a, w_rg_x=w_rg_x, b_rg_x=b_rg_x, lru_lambda=lru_lambda,
             w_b_out=w_b_out, w_c_out=w_c_out, w_mix_out=w_mix_out, ln1_g=ln1_g, ln1_b=ln1_b,
             w_ff_gate=w_ff_gate, w_ff_up=w_ff_up, w_ff_down=w_ff_down, w_router=w_router,
             b_router=b_router, w_e_gate=w_e_gate, w_e_up=w_e_up, w_e_down=w_e_down,
             ln2_g=ln2_g, ln2_b=ln2_b)
    bp, tp, _ = x_prompt.shape
    bs, ts, _ = x_sample.shape
    n_mem = mem_prompt.shape[1]

    xp_bm, xs_bm = x_prompt, x_sample
    yp = _to_tm(x_prompt)
    ys = _to_tm(x_sample)
    w_in_bf = w_in.astype(BF16)
    b_in3 = b_in.reshape(DEPTH, 1, D_IN)
    zero_a = jnp.zeros((K_A - 1, bp, D_A), F32)
    zero_b = jnp.zeros((K_B - 1, bp, D_B), F32)
    zero_h = jnp.zeros((bp, D_B), F32)
    zero_bias = jnp.zeros((DEPTH, 1, 2 * D_C), F32)
    cache_k = cache_mem_k.reshape(DEPTH, bs, n_mem * H_C, DH_C)
    cache_v = cache_mem_v.reshape(DEPTH, bs, n_mem * H_C, DH_C)
    outs = {k: [] for k in ("pa", "pb", "ph", "pk", "pv", "sa", "sb", "sh")}
    for l in range(DEPTH):
        w = _layer_weights(l, p)
        kv = matmul_bias(mem_prompt.reshape(bp * n_mem, D_MODEL), w_mem_kv, zero_bias, l,
                         Z_BLOCK, "kv_proj").reshape(bp, n_mem, 2 * D_C)
        mk = kv[..., :D_C]
        mv = kv[..., D_C:]
        kv4 = kv.reshape(1, bp, n_mem, 2 * D_C)
        if l > 0:
            xp_bm, xs_bm = _to_tm(yp), _to_tm(ys)
        op = attention(xp_bm, w_in_bf, b_in3, l, kv4, kv4, (0, 0, 1), tq=512)
        xp1, na, nb_, nh = _mix_layer(l, w, w_in_bf, yp, op, zero_a, zero_b, zero_h,
                                      tt=32, nb=bp, o_bm=True)
        outs["pa"].append(_to_tm(na))
        outs["pb"].append(_to_tm(nb_))
        outs["ph"].append(nh)
        outs["pk"].append(mk.reshape(bp, n_mem, H_C, DH_C))
        outs["pv"].append(mv.reshape(bp, n_mem, H_C, DH_C))
        os_ = _to_tm(attention_rows(xs_bm, w_in_bf, b_in3, l, cache_k, cache_v, sb=8))
        xs1, na, nb_, nh = _mix_layer(l, w, w_in_bf, ys, os_, _to_tm(state_conv_a[l]),
                                      _to_tm(state_conv_b[l]), state_rglru[l],
                                      tt=ts, nb=32, o_bm=False)
        outs["sa"].append(_to_tm(na))
        outs["sb"].append(_to_tm(nb_))
        outs["sh"].append(nh)
        if l % 2 == 0:
            j = l // 2
            xp2, xs2 = [ffn_dense(x, w_ff_gate, w_ff_up, w_ff_down, ln2_g[l], ln2_b[l], j)
                        for x in (xp1, xs1)]
        else:
            xp2, xs2 = moe_routed([xp1, xs1], p, l)
        yp = xp2.reshape(tp, bp, D_MODEL)
        ys = xs2.reshape(ts, bs, D_MODEL)

    st = lambda k: jnp.stack(outs[k])
    return (_to_tm(yp), _to_tm(ys), st("pa"), st("pb"), st("ph"), st("pk"), st("pv"),
            st("sa"), st("sb"), st("sh"))
```

```python
import functools

import jax
import jax.numpy as jnp
from jax import lax
from jax.experimental import pallas as pl
from jax.experimental.pallas import tpu as pltpu

D_MODEL = 1024
N_MEM = 256
D_A = 512
K_A = 31
D_B = 1024
K_B = 4
H_B = 8
BW_B = D_B // H_B
LRU_C = 8.0
D_C = 512
H_C = 4
DH_C = D_C // H_C
N_BRANCH = 3
D_IN = 2 * D_A + 2 * D_B + D_C + N_BRANCH * D_MODEL
D_FF = 2816
N_EXPERTS = 8
TOP_K = 2
DEPTH = 2
ALPHA = (2.0 * DEPTH) ** 0.25
LN_EPS = 1e-5

Z_BLOCK = 512
Q_BLOCK = (2 * D_A + 2 * D_B) // Z_BLOCK
ZM_COL = 2 * D_A + 2 * D_B + D_C

LANES = 128
VMEM_LIMIT = 56 * 1024 * 1024

TM = 1024
TF = 256
TOK = D_MODEL // LANES
assert TOK == 8

BF16 = jnp.bfloat16
F32 = jnp.float32


def _params(*sem):
    return pltpu.CompilerParams(dimension_semantics=sem, vmem_limit_bytes=VMEM_LIMIT)


def _layer_norm(x, g, b):
    mu = jnp.mean(x, axis=-1, keepdims=True)
    xc = x - mu
    var = jnp.mean(xc * xc, axis=-1, keepdims=True)
    return xc * lax.rsqrt(var + LN_EPS) * g + b


def _bdot(a, b):
    return jnp.dot(a.astype(BF16), b.astype(BF16), preferred_element_type=F32)


def _rows_from_seq_major(src_ref, stage_ref, nb):
    _, tt, c = src_ref.shape
    for s in range(nb):
        for k in range(c // LANES):
            stage_ref[k, pl.ds(s, tt, stride=nb), :] = src_ref[s, :, k * LANES:(k + 1) * LANES]
    return jnp.concatenate([stage_ref[k] for k in range(c // LANES)], axis=1)


def _seq_major_from_rows(dst_ref, stage_ref, rows, nb):
    r, c = rows.shape
    for k in range(c // LANES):
        stage_ref[k] = rows[:, k * LANES:(k + 1) * LANES]
    for s in range(nb):
        for k in range(c // LANES):
            dst_ref[s, :, k * LANES:(k + 1) * LANES] = stage_ref[k, pl.ds(s, r // nb, stride=nb), :]


def _matmul_bias_kernel(x_ref, w_ref, b_ref, o_ref, xb_ref):
    @pl.when(pl.program_id(1) == 0)
    def _():
        xb_ref[...] = x_ref[...].astype(BF16)

    o_ref[...] = jnp.dot(xb_ref[...], w_ref[...].astype(BF16),
                         preferred_element_type=F32) + b_ref[...]


def matmul_bias(x, w, b, l, tn, name):
    m, k = x.shape
    n = w.shape[2]
    return pl.pallas_call(
        _matmul_bias_kernel,
        grid=(m // TM, n // tn),
        in_specs=[pl.BlockSpec((TM, k), lambda i, j: (i, 0)),
                  pl.BlockSpec((None, k, tn), lambda i, j: (l, 0, j)),
                  pl.BlockSpec((None, 1, tn), lambda i, j: (l, 0, j))],
        out_specs=pl.BlockSpec((TM, tn), lambda i, j: (i, j)),
        out_shape=jax.ShapeDtypeStruct((m, n), F32),
        scratch_shapes=[pltpu.VMEM((TM, k), BF16)],
        compiler_params=_params("arbitrary", "arbitrary"),
        name=name,
    )(x, w, b)


def _attention_kernel(x_ref, wq_ref, bq_ref, k_ref, v_ref, o_ref):
    q = jnp.dot(x_ref[...].astype(BF16), wq_ref[...], preferred_element_type=F32) + bq_ref[...]
    for h in range(H_C):
        cols = slice(h * DH_C, (h + 1) * DH_C)
        kh = k_ref[:, cols].astype(BF16)
        vh = v_ref[:, cols].astype(BF16)
        sc = lax.dot_general(q[:, cols].astype(BF16), kh, (((1,), (1,)), ((), ())),
                             preferred_element_type=F32) * (DH_C ** -0.5)
        e = jnp.exp(sc - jnp.max(sc, axis=-1, keepdims=True))
        p = e * (1.0 / jnp.sum(e, axis=-1, keepdims=True))
        o_ref[:, cols] = jnp.dot(p.astype(BF16), vh, preferred_element_type=F32)


def attention(x, w_in_bf, b_in, l, k, v, kv_index, tq):
    s, t, _ = x.shape
    lk, kc, vc = kv_index
    return pl.pallas_call(
        _attention_kernel,
        grid=(s, t // tq),
        in_specs=[pl.BlockSpec((None, tq, D_MODEL), lambda i, j: (i, j, 0)),
                  pl.BlockSpec((None, D_MODEL, D_C), lambda i, j: (l, 0, Q_BLOCK)),
                  pl.BlockSpec((None, 1, D_C), lambda i, j: (l, 0, Q_BLOCK)),
                  pl.BlockSpec((None, None, N_MEM, D_C), lambda i, j: (lk, i, 0, kc)),
                  pl.BlockSpec((None, None, N_MEM, D_C), lambda i, j: (lk, i, 0, vc))],
        out_specs=pl.BlockSpec((None, tq, D_C), lambda i, j: (i, j, 0)),
        out_shape=jax.ShapeDtypeStruct((s, t, D_C), F32),
        compiler_params=_params("arbitrary", "arbitrary"),
        name="attention",
    )(x, w_in_bf, b_in, k, v)


def _attention_rows_kernel(x_ref, wq_ref, bq_ref, k_ref, v_ref, o_ref, *, sb, t):
    n_rows = H_C * t
    n_cols = N_MEM * H_C
    row_head = lax.broadcasted_iota(jnp.int32, (n_rows, n_cols), 0) // t
    col_head = lax.broadcasted_iota(jnp.int32, (n_rows, n_cols), 1) % H_C
    own = row_head == col_head
    qs = (jnp.dot(x_ref[...].reshape(sb * t, D_MODEL).astype(BF16), wq_ref[...],
                  preferred_element_type=F32) + bq_ref[...])
    for s in range(sb):
        q = qs[s * t:(s + 1) * t, :]
        qa = jnp.concatenate([q[:, h * DH_C:(h + 1) * DH_C] for h in range(H_C)], axis=0)
        sc = lax.dot_general(qa.astype(BF16), k_ref[s].astype(BF16), (((1,), (1,)), ((), ())),
                             preferred_element_type=F32) * (DH_C ** -0.5)
        sc = jnp.where(own, sc, -jnp.inf)
        e = jnp.exp(sc - jnp.max(sc, axis=-1, keepdims=True))
        p = e * (1.0 / jnp.sum(e, axis=-1, keepdims=True))
        oa = jnp.dot(p.astype(BF16), v_ref[s].astype(BF16), preferred_element_type=F32)
        o_ref[s] = jnp.concatenate([oa[h * t:(h + 1) * t, :] for h in range(H_C)], axis=1)


def attention_rows(x, w_in_bf, b_in, l, k, v, sb):
    s, t, _ = x.shape
    rows = N_MEM * H_C
    return pl.pallas_call(
        functools.partial(_attention_rows_kernel, sb=sb, t=t),
        grid=(s // sb,),
        in_specs=[pl.BlockSpec((sb, t, D_MODEL), lambda i: (i, 0, 0)),
                  pl.BlockSpec((None, D_MODEL, D_C), lambda i: (l, 0, Q_BLOCK)),
                  pl.BlockSpec((None, 1, D_C), lambda i: (l, 0, Q_BLOCK)),
                  pl.BlockSpec((None, sb, rows, DH_C), lambda i: (l, i, 0, 0)),
                  pl.BlockSpec((None, sb, rows, DH_C), lambda i: (l, i, 0, 0))],
        out_specs=pl.BlockSpec((sb, t, D_C), lambda i: (i, 0, 0)),
        out_shape=jax.ShapeDtypeStruct((s, t, D_C), F32),
        compiler_params=_params("arbitrary"),
        name="attention_rows",
    )(x, w_in_bf, b_in, k, v)


def _mix_kernel(o_ref, x_ref, sa_ref, sb_ref, h0_ref, bin_ref,
                wca_ref, bca_ref, lag_ref, lab_ref, wao_ref,
                wcb_ref, bcb_ref, wrg_ref, bra_ref, brx_ref, lam_ref, wbo_ref,
                wco_ref, wmo_ref, l1g_ref, l1b_ref, win_hbm,
                x1_ref, nsa_ref, nsb_ref, hl_ref,
                ua_buf, xb_buf, a_buf, u_buf, hs_buf, h_carry, o_buf, x_buf, win_buf, win_sem,
                *, l, tt, nb, nt, x_bm):
    r = tt * nb
    ha = (K_A - 1) * nb
    hb = (K_B - 1) * nb
    i = pl.program_id(1)

    @pl.when((pl.program_id(0) == 0) & (i == 0))
    def _():
        load = pltpu.make_async_copy(win_hbm.at[l], win_buf, win_sem)
        load.start()
        load.wait()

    @pl.when(i == 0)
    def _():
        ua_buf[0:ha, :] = sa_ref[...].reshape(ha, D_A)
        xb_buf[0:hb, :] = sb_ref[...].reshape(hb, D_B)
        h_carry[...] = h0_ref[...]

    if x_bm:
        x = _rows_from_seq_major(x_ref, x_buf, nb)
    else:
        x = x_ref[...].reshape(r, D_MODEL)
    xb = x.astype(BF16)

    def zcols(lo, hi):
        return (jnp.dot(xb, win_buf[:, lo:hi], preferred_element_type=F32) + bin_ref[:, lo:hi])

    za = zcols(0, 2 * D_A)
    ua_buf[ha:ha + r, :] = za[:, :D_A] * jax.nn.sigmoid(za[:, D_A:])
    c_a = jnp.broadcast_to(bca_ref[...], (r, D_A))
    for k in range(K_A):
        c_a = c_a + wca_ref[k:k + 1, :] * ua_buf[k * nb:k * nb + r, :]
    y_a = _bdot(jax.nn.silu(_layer_norm(c_a, lag_ref[...], lab_ref[...])), wao_ref[...])
    nsa_ref[...] = ua_buf[r:r + ha, :].reshape(K_A - 1, nb, D_A)

    xb_buf[hb:hb + r, :] = zcols(2 * D_A, 2 * D_A + D_B)
    c_b = jnp.broadcast_to(bcb_ref[...], (r, D_B))
    for k in range(K_B):
        c_b = c_b + wcb_ref[k:k + 1, :] * xb_buf[k * nb:k * nb + r, :]
    nsb_ref[...] = xb_buf[r:r + hb, :].reshape(K_B - 1, nb, D_B)
    c_bf = c_b.astype(BF16)
    ra, rx = [], []
    for h in range(H_B):
        gh = jnp.dot(c_bf[:, h * BW_B:(h + 1) * BW_B], wrg_ref[h], preferred_element_type=F32)
        ra.append(gh[:, :BW_B])
        rx.append(gh[:, BW_B:])
    gate_r = jax.nn.sigmoid(jnp.concatenate(ra, axis=1) + bra_ref[...])
    gate_i = jax.nn.sigmoid(jnp.concatenate(rx, axis=1) + brx_ref[...])
    log_a = (-LRU_C) * gate_r * jax.nn.softplus(-lam_ref[...])
    a_buf[...] = jnp.exp(log_a)
    th = jnp.tanh(log_a)
    u_buf[...] = jnp.sqrt(-2.0 * th / (1.0 - th)) * (gate_i * c_b)
    h = h_carry[...]
    for t in range(tt):
        rows = slice(t * nb, (t + 1) * nb)
        h = a_buf[rows, :] * h + u_buf[rows, :]
        hs_buf[rows, :] = h
    h_carry[...] = h
    hl_ref[...] = h
    z_g = zcols(2 * D_A + D_B, 2 * D_A + 2 * D_B)
    y_b = _bdot(hs_buf[...] * jax.nn.gelu(z_g), wbo_ref[...])

    if nt > 1:
        ua_buf[0:ha, :] = ua_buf[r:r + ha, :]
        xb_buf[0:hb, :] = xb_buf[r:r + hb, :]

    y_c = _bdot(_rows_from_seq_major(o_ref, o_buf, nb), wco_ref[...])

    def gate(branch):
        lo = ZM_COL + branch * D_MODEL
        return jax.nn.sigmoid(zcols(lo, lo + D_MODEL))

    merged = gate(0) * y_a
    merged = merged + gate(1) * y_b
    merged = merged + gate(2) * y_c
    mix = _bdot(merged, wmo_ref[...])
    x1_ref[...] = _layer_norm(ALPHA * x + mix, l1g_ref[...], l1b_ref[...]).reshape(tt, nb, D_MODEL)


def mix(l, o, x, sa_tm, sb_tm, h0, w, w_in_bf, tt, nb, x_bm):
    nbt, t = o.shape[:2]
    nt = t // tt
    assert nt == 1 or tt >= K_A - 1
    r = tt * nb
    grid = (nbt // nb, nt)

    def bm_spec(width):
        return pl.BlockSpec((nb, tt, width), lambda g, i: (g, i, 0))

    def zspec(width, blk):
        return pl.BlockSpec((tt, nb, width), lambda g, i: (i, g, blk))

    def full(a):
        nd = a.ndim
        return pl.BlockSpec(a.shape, lambda g, i: (0,) * nd)

    weights = [w["b_in"], w["w_conv_a"], w["b_conv_a"], w["ln_a_g"], w["ln_a_b"], w["w_a_out"],
               w["w_conv_b"], w["b_conv_b"], w["w_rg"], w["b_rg_a"], w["b_rg_x"], w["lam"],
               w["w_b_out"], w["w_c_out"], w["w_mix_out"], w["ln1_g"], w["ln1_b"]]
    in_specs = ([bm_spec(D_C), bm_spec(D_MODEL) if x_bm else zspec(D_MODEL, 0),
                 pl.BlockSpec((K_A - 1, nb, D_A), lambda g, i: (0, g, 0)),
                 pl.BlockSpec((K_B - 1, nb, D_B), lambda g, i: (0, g, 0)),
                 pl.BlockSpec((nb, D_B), lambda g, i: (g, 0))]
                + [full(a) for a in weights]
                + [pl.BlockSpec(memory_space=pl.ANY)])
    out_specs = [zspec(D_MODEL, 0),
                 pl.BlockSpec((K_A - 1, nb, D_A), lambda g, i: (0, g, 0)),
                 pl.BlockSpec((K_B - 1, nb, D_B), lambda g, i: (0, g, 0)),
                 pl.BlockSpec((nb, D_B), lambda g, i: (g, 0))]
    out_shape = [jax.ShapeDtypeStruct((t, nbt, D_MODEL), F32),
                 jax.ShapeDtypeStruct((K_A - 1, nbt, D_A), F32),
                 jax.ShapeDtypeStruct((K_B - 1, nbt, D_B), F32),
                 jax.ShapeDtypeStruct((nbt, D_B), F32)]
    scratch = [pltpu.VMEM(((K_A - 1) * nb + r, D_A), F32),
               pltpu.VMEM(((K_B - 1) * nb + r, D_B), F32),
               pltpu.VMEM((r, D_B), F32), pltpu.VMEM((r, D_B), F32), pltpu.VMEM((r, D_B), F32),
               pltpu.VMEM((nb, D_B), F32), pltpu.VMEM((D_C // LANES, r, LANES), F32),
               pltpu.VMEM((D_MODEL // LANES, r, LANES), F32),
               pltpu.VMEM((D_MODEL, D_IN), BF16), pltpu.SemaphoreType.DMA(())]
    return pl.pallas_call(
        functools.partial(_mix_kernel, l=l, tt=tt, nb=nb, nt=nt, x_bm=x_bm),
        grid=grid, in_specs=in_specs, out_specs=out_specs, out_shape=out_shape,
        scratch_shapes=scratch,
        compiler_params=_params("arbitrary", "arbitrary"),
        name="mix",
    )(o, x, sa_tm, sb_tm, h0, *weights, w_in_bf)


def _ffn_kernel(x_ref, wg_ref, wu_ref, wd_ref, g_ref, b_ref, o_ref, obm_ref, xb_ref, acc_ref,
                stage_ref, *, nb_seq):
    j = pl.program_id(1)

    @pl.when(j == 0)
    def _():
        xb_ref[...] = x_ref[...].astype(BF16)
        acc_ref[...] = jnp.zeros_like(acc_ref)

    xb = xb_ref[...]
    hg = jnp.dot(xb, wg_ref[...].astype(BF16), preferred_element_type=F32)
    hu = jnp.dot(xb, wu_ref[...].astype(BF16), preferred_element_type=F32)
    acc_ref[...] += _bdot(jax.nn.silu(hg) * hu, wd_ref[...])

    @pl.when(j == pl.num_programs(1) - 1)
    def _():
        res = _layer_norm(ALPHA * x_ref[...] + acc_ref[...], g_ref[...], b_ref[...])
        o_ref[...] = res
        _seq_major_from_rows(obm_ref, stage_ref, res, nb_seq)


def ffn_dense(x, wg, wu, wd, g, b, l, nb_seq):
    m = x.shape[0]
    return pl.pallas_call(
        functools.partial(_ffn_kernel, nb_seq=nb_seq),
        grid=(m // TM, D_FF // TF),
        in_specs=[pl.BlockSpec((TM, D_MODEL), lambda i, j: (i, 0)),
                  pl.BlockSpec((None, D_MODEL, TF), lambda i, j: (l, 0, j)),
                  pl.BlockSpec((None, D_MODEL, TF), lambda i, j: (l, 0, j)),
                  pl.BlockSpec((None, TF, D_MODEL), lambda i, j: (l, j, 0)),
                  pl.BlockSpec((1, D_MODEL), lambda i, j: (0, 0)),
                  pl.BlockSpec((1, D_MODEL), lambda i, j: (0, 0))],
        out_specs=[pl.BlockSpec((TM, D_MODEL), lambda i, j: (i, 0)),
                   pl.BlockSpec((nb_seq, TM // nb_seq, D_MODEL), lambda i, j: (0, i, 0))],
        out_shape=[jax.ShapeDtypeStruct((m, D_MODEL), F32),
                   jax.ShapeDtypeStruct((nb_seq, m // nb_seq, D_MODEL), F32)],
        scratch_shapes=[pltpu.VMEM((TM, D_MODEL), BF16), pltpu.VMEM((TM, D_MODEL), F32),
                        pltpu.VMEM((TOK, TM, LANES), F32)],
        compiler_params=_params("arbitrary", "arbitrary"),
        name="ffn_dense",
    )(x, wg, wu, wd, g.reshape(1, -1), b.reshape(1, -1))


def _router_kernel(x_ref, w_ref, b_ref, sel_ref, prob_ref):
    logits = jnp.dot(x_ref[...], w_ref[...], precision=lax.Precision.HIGHEST,
                     preferred_element_type=F32) + b_ref[...]
    lane = lax.broadcasted_iota(jnp.int32, logits.shape, 1)
    neg = jnp.float32(-jnp.inf)
    l1 = jnp.where(lane < N_EXPERTS, logits, neg)
    m1 = jnp.max(l1, axis=1, keepdims=True)
    i1 = jnp.min(jnp.where(l1 == m1, lane, LANES), axis=1, keepdims=True)
    l2 = jnp.where(lane == i1, neg, l1)
    m2 = jnp.max(l2, axis=1, keepdims=True)
    i2 = jnp.min(jnp.where(l2 == m2, lane, LANES), axis=1, keepdims=True)
    e2 = jnp.exp(m2 - m1)
    den = 1.0 + e2
    sel_ref[...] = jnp.where(lane == 0, i1, jnp.where(lane == 1, i2, 0))
    prob_ref[...] = jnp.where(lane == 0, 1.0 / den, jnp.where(lane == 1, e2 / den, 0.0))


def router(x, w_router, b_router):
    m = x.shape[0]
    wp = jnp.zeros((D_MODEL, LANES), F32).at[:, :N_EXPERTS].set(w_router)
    bp = jnp.zeros((1, LANES), F32).at[0, :N_EXPERTS].set(b_router)
    return pl.pallas_call(
        _router_kernel,
        grid=(m // TM,),
        in_specs=[pl.BlockSpec((TM, D_MODEL), lambda i: (i, 0)),
                  pl.BlockSpec((D_MODEL, LANES), lambda i: (0, 0)),
                  pl.BlockSpec((1, LANES), lambda i: (0, 0))],
        out_specs=[pl.BlockSpec((TM, LANES), lambda i: (i, 0)),
                   pl.BlockSpec((TM, LANES), lambda i: (i, 0))],
        out_shape=[jax.ShapeDtypeStruct((m, LANES), jnp.int32),
                   jax.ShapeDtypeStruct((m, LANES), F32)],
        compiler_params=_params("arbitrary"),
        name="router",
    )(x, wp, bp)


def _route(sel, tm, n_tiles):
    e = sel[:, :TOP_K].reshape(-1)
    onehot = (e[:, None] == jnp.arange(N_EXPERTS, dtype=jnp.int32)[None, :]).astype(jnp.int32)
    csum = jnp.cumsum(onehot, axis=0)
    rank = jnp.sum(onehot * csum, axis=1) - 1
    counts = csum[-1]
    padded = ((counts + tm - 1) // tm) * tm
    ends = jnp.cumsum(padded)
    starts = ends - padded
    dest = jnp.sum(onehot * starts[None, :], axis=1) + rank
    first_row = jnp.arange(n_tiles, dtype=jnp.int32) * tm
    tile_expert = jnp.sum((first_row[:, None] >= ends[None, :]).astype(jnp.int32), axis=1)
    tile_expert = jnp.minimum(tile_expert, N_EXPERTS - 1)
    n_used = (ends[-1] // tm).reshape(1)
    return dest.astype(jnp.int32), tile_expert.astype(jnp.int32), n_used.astype(jnp.int32)


def _token_copy(src_ref, dst_ref, sem, s_tok, d_tok, n_tok):
    s0 = pl.multiple_of(s_tok * TOK, TOK)
    d0 = pl.multiple_of(d_tok * TOK, TOK)
    return pltpu.make_async_copy(src_ref.at[pl.ds(s0, n_tok * TOK), :],
                                 dst_ref.at[pl.ds(d0, n_tok * TOK), :], sem)


def _to_token_tiles(dst_ref, src, rows):
    for s in range(TOK):
        dst_ref[pl.ds(s, rows, stride=TOK), :] = src[:, s * LANES:(s + 1) * LANES]


def _dispatch_kernel(didx_ref, x_ref, init_ref, out_ref, tok_buf, sem):
    del init_ref
    _to_token_tiles(tok_buf, x_ref[...], TM)

    def issue(t, carry):
        for k in range(TOP_K):
            _token_copy(tok_buf, out_ref, sem, t, didx_ref[0, TOP_K * t + k], 1).start()
        return carry

    lax.fori_loop(0, TM, issue, 0)
    for _ in range(TOP_K):
        _token_copy(tok_buf, out_ref, sem, 0, 0, TM).wait()


def dispatch(x, dest, grouped):
    m = x.shape[0]
    nc = m // TM
    return pl.pallas_call(
        _dispatch_kernel,
        grid=(nc,),
        in_specs=[pl.BlockSpec((None, 1, TOP_K * TM), lambda c: (c, 0, 0),
                               memory_space=pltpu.SMEM),
                  pl.BlockSpec((TM, D_MODEL), lambda c: (c, 0)),
                  pl.BlockSpec(memory_space=pl.ANY)],
        out_specs=pl.BlockSpec(memory_space=pl.ANY),
        out_shape=jax.ShapeDtypeStruct(grouped.shape, grouped.dtype),
        scratch_shapes=[pltpu.VMEM((TM * TOK, LANES), F32), pltpu.SemaphoreType.DMA(())],
        input_output_aliases={2: 0},
        compiler_params=_params("arbitrary"),
        name="dispatch",
    )(dest.reshape(nc, 1, TOP_K * TM), x, grouped)


def _gmm_kernel(te_ref, nu_ref, x_ref, wg_ref, wu_ref, wd_ref, y_ref, xb_ref, acc_ref):
    del te_ref
    i = pl.program_id(0)
    j = pl.program_id(1)

    @pl.when((i >= nu_ref[0]) & (j == 0))
    def _():
        y_ref[...] = jnp.zeros_like(y_ref)

    @pl.when(i < nu_ref[0])
    def _():
        @pl.when(j == 0)
        def _():
            for s in range(TOK):
                xb_ref[:, s * LANES:(s + 1) * LANES] = (
                    x_ref[pl.ds(s, TM, stride=TOK), :].astype(BF16))
            acc_ref[...] = jnp.zeros_like(acc_ref)

        xb = xb_ref[...]
        hg = jnp.dot(xb, wg_ref[...].astype(BF16), preferred_element_type=F32)
        hu = jnp.dot(xb, wu_ref[...].astype(BF16), preferred_element_type=F32)
        acc_ref[...] += _bdot(jax.nn.silu(hg) * hu, wd_ref[...])

        @pl.when(j == pl.num_programs(1) - 1)
        def _():
            _to_token_tiles(y_ref, acc_ref, TM)


def gmm(xs, tile_expert, n_used, wg, wu, wd, l):
    rows = xs.shape[0] // TOK
    n_tiles = rows // TM
    nf = D_FF // TF

    def tile(i, nu):
        return jnp.minimum(i, nu[0] - 1)

    def ff(i, j, nu):
        return jnp.where(i < nu[0], j, nf - 1)

    grid_spec = pltpu.PrefetchScalarGridSpec(
        num_scalar_prefetch=2,
        grid=(n_tiles, nf),
        in_specs=[pl.BlockSpec((TM * TOK, LANES), lambda i, j, te, nu: (i, 0)),
                  pl.BlockSpec((None, None, D_MODEL, TF),
                               lambda i, j, te, nu: (l, te[tile(i, nu)], 0, ff(i, j, nu))),
                  pl.BlockSpec((None, None, D_MODEL, TF),
                               lambda i, j, te, nu: (l, te[tile(i, nu)], 0, ff(i, j, nu))),
                  pl.BlockSpec((None, None, TF, D_MODEL),
                               lambda i, j, te, nu: (l, te[tile(i, nu)], ff(i, j, nu), 0))],
        out_specs=pl.BlockSpec((TM * TOK, LANES), lambda i, j, te, nu: (i, 0)),
        scratch_shapes=[pltpu.VMEM((TM, D_MODEL), BF16), pltpu.VMEM((TM, D_MODEL), F32)],
    )
    return pl.pallas_call(
        _gmm_kernel,
        grid_spec=grid_spec,
        out_shape=jax.ShapeDtypeStruct(xs.shape, F32),
        compiler_params=_params("arbitrary", "arbitrary"),
        name="gmm",
    )(tile_expert, n_used, xs, wg, wu, wd)


def _combine_kernel(didx_ref, x_ref, prob_ref, g_ref, b_ref, y_ref, o_ref, ybuf, stage_ref, sem,
                    *, nb_seq):
    def issue(t, carry):
        for k in range(TOP_K):
            _token_copy(y_ref, ybuf, sem, didx_ref[0, TOP_K * t + k], k * TM + t, 1).start()
        return carry

    lax.fori_loop(0, TM, issue, 0)
    for k in range(TOP_K):
        _token_copy(y_ref, ybuf, sem, 0, k * TM, TM).wait()
    p0 = prob_ref[:, 0:1]
    p1 = prob_ref[:, 1:2]
    f = jnp.concatenate(
        [p0 * ybuf[pl.ds(s, TM, stride=TOK), :] + p1 * ybuf[pl.ds(TM * TOK + s, TM, stride=TOK), :]
         for s in range(TOK)], axis=1)
    res = _layer_norm(ALPHA * x_ref[...] + f, g_ref[...], b_ref[...])
    _seq_major_from_rows(o_ref, stage_ref, res, nb_seq)


def combine(x, y, dest, prob, g, b, nb_seq):
    m = x.shape[0]
    nc = m // TM
    return pl.pallas_call(
        functools.partial(_combine_kernel, nb_seq=nb_seq),
        grid=(nc,),
        in_specs=[pl.BlockSpec((None, 1, TOP_K * TM), lambda i: (i, 0, 0),
                               memory_space=pltpu.SMEM),
                  pl.BlockSpec((TM, D_MODEL), lambda i: (i, 0)),
                  pl.BlockSpec((TM, LANES), lambda i: (i, 0)),
                  pl.BlockSpec((1, D_MODEL), lambda i: (0, 0)),
                  pl.BlockSpec((1, D_MODEL), lambda i: (0, 0)),
                  pl.BlockSpec(memory_space=pl.ANY)],
        out_specs=pl.BlockSpec((nb_seq, TM // nb_seq, D_MODEL), lambda i: (0, i, 0)),
        out_shape=jax.ShapeDtypeStruct((nb_seq, m // nb_seq, D_MODEL), F32),
        scratch_shapes=[pltpu.VMEM((TOP_K * TM * TOK, LANES), F32),
                        pltpu.VMEM((TOK, TM, LANES), F32), pltpu.SemaphoreType.DMA(())],
        compiler_params=_params("arbitrary"),
        name="combine",
    )(dest.reshape(nc, 1, TOP_K * TM), x, prob, g.reshape(1, -1), b.reshape(1, -1), y)


def moe_routed(xs_list, nb_seqs, p, l):
    j = l // 2
    routed = [router(x, p["w_router"][j], p["b_router"][j]) for x in xs_list]
    sel = jnp.concatenate([r[0] for r in routed], axis=0)
    n_pairs = TOP_K * sel.shape[0]
    n_tiles = n_pairs // TM + N_EXPERTS
    dest, tile_expert, n_used = _route(sel, TM, n_tiles)
    bounds = [0]
    for x in xs_list:
        bounds.append(bounds[-1] + TOP_K * x.shape[0])

    grouped = jnp.zeros((n_tiles * TM * TOK, LANES), F32)
    for x, lo, hi in zip(xs_list, bounds[:-1], bounds[1:]):
        grouped = dispatch(x, dest[lo:hi], grouped)
    y = gmm(grouped, tile_expert, n_used, p["w_e_gate"], p["w_e_up"], p["w_e_down"], j)
    return [combine(x, y, dest[lo:hi], prob, p["ln2_g"][l], p["ln2_b"][l], nb)
            for x, nb, (_, prob), lo, hi in zip(xs_list, nb_seqs, routed, bounds[:-1], bounds[1:])]


def _layer_weights(l, p):
    row = lambda a: a[l].reshape(1, -1)
    return {
        "b_in": row(p["b_in"]),
        "w_conv_a": p["w_conv_a"][l], "b_conv_a": row(p["b_conv_a"]),
        "ln_a_g": row(p["ln_a_g"]), "ln_a_b": row(p["ln_a_b"]),
        "w_a_out": p["w_a_out"][l].astype(BF16),
        "w_conv_b": p["w_conv_b"][l], "b_conv_b": row(p["b_conv_b"]),
        "w_rg": jnp.concatenate([p["w_rg_a"][l], p["w_rg_x"][l]], axis=-1).astype(BF16),
        "b_rg_a": row(p["b_rg_a"]), "b_rg_x": row(p["b_rg_x"]), "lam": row(p["lru_lambda"]),
        "w_b_out": p["w_b_out"][l].astype(BF16), "w_c_out": p["w_c_out"][l].astype(BF16),
        "w_mix_out": p["w_mix_out"][l].astype(BF16),
        "ln1_g": row(p["ln1_g"]), "ln1_b": row(p["ln1_b"]),
    }


def _mix_layer(l, w, w_in_bf, x, o, sa_tm, sb_tm, h0, tt, nb, x_bm):
    nbt, t = o.shape[:2]
    x1, nsa, nsb, hl = mix(l, o, x, sa_tm, sb_tm, h0, w, w_in_bf, tt, nb, x_bm)
    return x1.reshape(t * nbt, D_MODEL), nsa, nsb, hl


def _to_tm(a):
    return jnp.transpose(a, (1, 0, 2))


def kernel(x_prompt, x_sample, state_conv_a, state_conv_b, state_rglru, cache_mem_k, cache_mem_v, mem_prompt, w_in, b_in, w_conv_a, b_conv_a, ln_a_g, ln_a_b, w_a_out, w_conv_b, b_conv_b, w_rg_a, b_rg_a, w_rg_x, b_rg_x, lru_lambda, w_b_out, w_mem_kv, w_c_out, w_mix_out, ln1_g, ln1_b, w_ff_gate, w_ff_up, w_ff_down, w_router, b_router, w_e_gate, w_e_up, w_e_down, ln2_g, ln2_b):
    p = dict(w_in=w_in, b_in=b_in, w_conv_a=w_conv_a, b_conv_a=b_conv_a, ln_a_g=ln_a_g,
             ln_a_b=ln_a_b, w_a_out=w_a_out, w_conv_b=w_conv_b, b_conv_b=b_conv_b,
             w_rg_a=w_rg_a, b_rg_a=b_rg_a, w_rg_x=w_rg_x, b_rg_x=b_rg_x, lru_lambda=lru_lambda,
             w_b_out=w_b_out, w_c_out=w_c_out, w_mix_out=w_mix_out, ln1_g=ln1_g, ln1_b=ln1_b,
             w_ff_gate=w_ff_gate, w_ff_up=w_ff_up, w_ff_down=w_ff_down, w_router=w_router,
             b_router=b_router, w_e_gate=w_e_gate, w_e_up=w_e_up, w_e_down=w_e_down,
             ln2_g=ln2_g, ln2_b=ln2_b)
    bp, tp, _ = x_prompt.shape
    bs, ts, _ = x_sample.shape
    n_mem = mem_prompt.shape[1]

    xp_bm, xs_bm = x_prompt, x_sample
    xp_tm = xs_tm = None
    w_in_bf = w_in.astype(BF16)
    b_in3 = b_in.reshape(DEPTH, 1, D_IN)
    zero_a = jnp.zeros((K_A - 1, bp, D_A), F32)
    zero_b = jnp.zeros((K_B - 1, bp, D_B), F32)
    zero_h = jnp.zeros((bp, D_B), F32)
    zero_bias = jnp.zeros((DEPTH, 1, 2 * D_C), F32)
    cache_k = cache_mem_k.reshape(DEPTH, bs, n_mem * H_C, DH_C)
    cache_v = cache_mem_v.reshape(DEPTH, bs, n_mem * H_C, DH_C)
    outs = {k: [] for k in ("pa", "pb", "ph", "pk", "pv", "sa", "sb", "sh")}
    for l in range(DEPTH):
        w = _layer_weights(l, p)
        kv = matmul_bias(mem_prompt.reshape(bp * n_mem, D_MODEL), w_mem_kv, zero_bias, l,
                         Z_BLOCK, "kv_proj").reshape(bp, n_mem, 2 * D_C)
        mk = kv[..., :D_C]
        mv = kv[..., D_C:]
        kv4 = kv.reshape(1, bp, n_mem, 2 * D_C)
        op = attention(xp_bm, w_in_bf, b_in3, l, kv4, kv4, (0, 0, 1), tq=512)
        xp1, na, nb_, nh = _mix_layer(l, w, w_in_bf, xp_bm if xp_tm is None else xp_tm, op,
                                      zero_a, zero_b, zero_h, tt=32, nb=bp, x_bm=xp_tm is None)
        outs["pa"].append(_to_tm(na))
        outs["pb"].append(_to_tm(nb_))
        outs["ph"].append(nh)
        outs["pk"].append(mk.reshape(bp, n_mem, H_C, DH_C))
        outs["pv"].append(mv.reshape(bp, n_mem, H_C, DH_C))
        os_ = attention_rows(xs_bm, w_in_bf, b_in3, l, cache_k, cache_v, sb=8)
        xs1, na, nb_, nh = _mix_layer(l, w, w_in_bf, xs_bm if xs_tm is None else xs_tm, os_,
                                      _to_tm(state_conv_a[l]), _to_tm(state_conv_b[l]),
                                      state_rglru[l], tt=ts, nb=32, x_bm=xs_tm is None)
        outs["sa"].append(_to_tm(na))
        outs["sb"].append(_to_tm(nb_))
        outs["sh"].append(nh)
        if l % 2 == 0:
            j = l // 2
            (xp_rows, xp_bm), (xs_rows, xs_bm) = [
                ffn_dense(x, w_ff_gate, w_ff_up, w_ff_down, ln2_g[l], ln2_b[l], j, nb)
                for x, nb in ((xp1, bp), (xs1, bs))]
            xp_tm = xp_rows.reshape(tp, bp, D_MODEL)
            xs_tm = xs_rows.reshape(ts, bs, D_MODEL)
        else:
            xp_bm, xs_bm = moe_routed([xp1, xs1], [bp, bs], p, l)
            xp_tm = xs_tm = None

    st = lambda k: jnp.stack(outs[k])
    return (xp_bm, xs_bm, st("pa"), st("pb"), st("ph"), st("pk"), st("pv"),
            st("sa"), st("sb"), st("sh"))
```

```python
import functools

import jax
import jax.numpy as jnp
from jax import lax
from jax.experimental import pallas as pl
from jax.experimental.pallas import tpu as pltpu

D_MODEL = 1024
N_MEM = 256
D_A = 512
K_A = 31
D_B = 1024
K_B = 4
H_B = 8
BW_B = D_B // H_B
LRU_C = 8.0
D_C = 512
H_C = 4
DH_C = D_C // H_C
N_BRANCH = 3
D_IN = 2 * D_A + 2 * D_B + D_C + N_BRANCH * D_MODEL
D_FF = 2816
N_EXPERTS = 8
TOP_K = 2
DEPTH = 2
ALPHA = (2.0 * DEPTH) ** 0.25
LN_EPS = 1e-5

Z_BLOCK = 512
Q_BLOCK = (2 * D_A + 2 * D_B) // Z_BLOCK
ZM_COL = 2 * D_A + 2 * D_B + D_C

LANES = 128
VMEM_LIMIT = 56 * 1024 * 1024

TM = 1024
TF = 256
N_DMA_PRIORITIES = 2
TOK = D_MODEL // LANES
assert TOK == 8

BF16 = jnp.bfloat16
F32 = jnp.float32


def _params(*sem):
    return pltpu.CompilerParams(dimension_semantics=sem, vmem_limit_bytes=VMEM_LIMIT)


def _layer_norm(x, g, b):
    mu = jnp.mean(x, axis=-1, keepdims=True)
    xc = x - mu
    var = jnp.mean(xc * xc, axis=-1, keepdims=True)
    return xc * lax.rsqrt(var + LN_EPS) * g + b


def _bdot(a, b):
    return jnp.dot(a.astype(BF16), b.astype(BF16), preferred_element_type=F32)


def _rows_from_seq_major(src_ref, stage_ref, nb):
    _, tt, c = src_ref.shape
    for s in range(nb):
        for k in range(c // LANES):
            stage_ref[k, pl.ds(s, tt, stride=nb), :] = src_ref[s, :, k * LANES:(k + 1) * LANES]
    return jnp.concatenate([stage_ref[k] for k in range(c // LANES)], axis=1)


def _seq_major_from_rows(dst_ref, stage_ref, rows, nb):
    r, c = rows.shape
    for k in range(c // LANES):
        stage_ref[k] = rows[:, k * LANES:(k + 1) * LANES]
    for s in range(nb):
        for k in range(c // LANES):
            dst_ref[s, :, k * LANES:(k + 1) * LANES] = stage_ref[k, pl.ds(s, r // nb, stride=nb), :]


def _matmul_bias_kernel(x_ref, w_ref, b_ref, o_ref, xb_ref):
    @pl.when(pl.program_id(1) == 0)
    def _():
        xb_ref[...] = x_ref[...].astype(BF16)

    o_ref[...] = jnp.dot(xb_ref[...], w_ref[...].astype(BF16),
                         preferred_element_type=F32) + b_ref[...]


def matmul_bias(x, w, b, l, tn, name):
    m, k = x.shape
    n = w.shape[2]
    return pl.pallas_call(
        _matmul_bias_kernel,
        grid=(m // TM, n // tn),
        in_specs=[pl.BlockSpec((TM, k), lambda i, j: (i, 0)),
                  pl.BlockSpec((None, k, tn), lambda i, j: (l, 0, j)),
                  pl.BlockSpec((None, 1, tn), lambda i, j: (l, 0, j))],
        out_specs=pl.BlockSpec((TM, tn), lambda i, j: (i, j)),
        out_shape=jax.ShapeDtypeStruct((m, n), F32),
        scratch_shapes=[pltpu.VMEM((TM, k), BF16)],
        compiler_params=_params("arbitrary", "arbitrary"),
        name=name,
    )(x, w, b)


def _attention_kernel(x_ref, wq_ref, bq_ref, k_ref, v_ref, o_ref):
    q = jnp.dot(x_ref[...].astype(BF16), wq_ref[...], preferred_element_type=F32) + bq_ref[...]
    for h in range(H_C):
        cols = slice(h * DH_C, (h + 1) * DH_C)
        kh = k_ref[:, cols].astype(BF16)
        vh = v_ref[:, cols].astype(BF16)
        sc = lax.dot_general(q[:, cols].astype(BF16), kh, (((1,), (1,)), ((), ())),
                             preferred_element_type=F32) * (DH_C ** -0.5)
        e = jnp.exp(sc - jnp.max(sc, axis=-1, keepdims=True))
        p = e * (1.0 / jnp.sum(e, axis=-1, keepdims=True))
        o_ref[:, cols] = jnp.dot(p.astype(BF16), vh, preferred_element_type=F32)


def attention(x, w_in_bf, b_in, l, k, v, kv_index, tq):
    s, t, _ = x.shape
    lk, kc, vc = kv_index
    return pl.pallas_call(
        _attention_kernel,
        grid=(s, t // tq),
        in_specs=[pl.BlockSpec((None, tq, D_MODEL), lambda i, j: (i, j, 0)),
                  pl.BlockSpec((None, D_MODEL, D_C), lambda i, j: (l, 0, Q_BLOCK)),
                  pl.BlockSpec((None, 1, D_C), lambda i, j: (l, 0, Q_BLOCK)),
                  pl.BlockSpec((None, None, N_MEM, D_C), lambda i, j: (lk, i, 0, kc)),
                  pl.BlockSpec((None, None, N_MEM, D_C), lambda i, j: (lk, i, 0, vc))],
        out_specs=pl.BlockSpec((None, tq, D_C), lambda i, j: (i, j, 0)),
        out_shape=jax.ShapeDtypeStruct((s, t, D_C), F32),
        compiler_params=_params("arbitrary", "arbitrary"),
        name="attention",
    )(x, w_in_bf, b_in, k, v)


def _attention_rows_kernel(x_ref, wq_ref, bq_ref, k_ref, v_ref, o_ref, *, sb, t):
    n_rows = H_C * t
    n_cols = N_MEM * H_C
    row_head = lax.broadcasted_iota(jnp.int32, (n_rows, n_cols), 0) // t
    col_head = lax.broadcasted_iota(jnp.int32, (n_rows, n_cols), 1) % H_C
    own = row_head == col_head
    qs = (jnp.dot(x_ref[...].reshape(sb * t, D_MODEL).astype(BF16), wq_ref[...],
                  preferred_element_type=F32) + bq_ref[...])
    for s in range(sb):
        q = qs[s * t:(s + 1) * t, :]
        qa = jnp.concatenate([q[:, h * DH_C:(h + 1) * DH_C] for h in range(H_C)], axis=0)
        sc = lax.dot_general(qa.astype(BF16), k_ref[s].astype(BF16), (((1,), (1,)), ((), ())),
                             preferred_element_type=F32) * (DH_C ** -0.5)
        sc = jnp.where(own, sc, -jnp.inf)
        e = jnp.exp(sc - jnp.max(sc, axis=-1, keepdims=True))
        p = e * (1.0 / jnp.sum(e, axis=-1, keepdims=True))
        oa = jnp.dot(p.astype(BF16), v_ref[s].astype(BF16), preferred_element_type=F32)
        o_ref[s] = jnp.concatenate([oa[h * t:(h + 1) * t, :] for h in range(H_C)], axis=1)


def attention_rows(x, w_in_bf, b_in, l, k, v, sb):
    s, t, _ = x.shape
    rows = N_MEM * H_C
    return pl.pallas_call(
        functools.partial(_attention_rows_kernel, sb=sb, t=t),
        grid=(s // sb,),
        in_specs=[pl.BlockSpec((sb, t, D_MODEL), lambda i: (i, 0, 0)),
                  pl.BlockSpec((None, D_MODEL, D_C), lambda i: (l, 0, Q_BLOCK)),
                  pl.BlockSpec((None, 1, D_C), lambda i: (l, 0, Q_BLOCK)),
                  pl.BlockSpec((None, sb, rows, DH_C), lambda i: (l, i, 0, 0)),
                  pl.BlockSpec((None, sb, rows, DH_C), lambda i: (l, i, 0, 0))],
        out_specs=pl.BlockSpec((sb, t, D_C), lambda i: (i, 0, 0)),
        out_shape=jax.ShapeDtypeStruct((s, t, D_C), F32),
        compiler_params=_params("arbitrary"),
        name="attention_rows",
    )(x, w_in_bf, b_in, k, v)


def _mix_kernel(o_ref, x_ref, sa_ref, sb_ref, h0_ref, bin_ref,
                wca_ref, bca_ref, lag_ref, lab_ref, wao_ref,
                wcb_ref, bcb_ref, wrg_ref, bra_ref, brx_ref, lam_ref, wbo_ref,
                wco_ref, wmo_ref, l1g_ref, l1b_ref, win_hbm,
                x1_ref, nsa_ref, nsb_ref, hl_ref,
                ua_buf, xb_buf, a_buf, u_buf, hs_buf, h_carry, o_buf, x_buf, win_buf, win_sem,
                *, l, tt, nb, nt, x_bm):
    r = tt * nb
    ha = (K_A - 1) * nb
    hb = (K_B - 1) * nb
    i = pl.program_id(1)

    @pl.when((pl.program_id(0) == 0) & (i == 0))
    def _():
        load = pltpu.make_async_copy(win_hbm.at[l], win_buf, win_sem)
        load.start()
        load.wait()

    @pl.when(i == 0)
    def _():
        ua_buf[0:ha, :] = sa_ref[...].reshape(ha, D_A)
        xb_buf[0:hb, :] = sb_ref[...].reshape(hb, D_B)
        h_carry[...] = h0_ref[...]

    if x_bm:
        x = _rows_from_seq_major(x_ref, x_buf, nb)
    else:
        x = x_ref[...].reshape(r, D_MODEL)
    xb = x.astype(BF16)

    def zcols(lo, hi):
        return (jnp.dot(xb, win_buf[:, lo:hi], preferred_element_type=F32) + bin_ref[:, lo:hi])

    za = zcols(0, 2 * D_A)
    ua_buf[ha:ha + r, :] = za[:, :D_A] * jax.nn.sigmoid(za[:, D_A:])
    later_cols = [(2 * D_A + n * D_B, 2 * D_A + (n + 1) * D_B) for n in range(2)]
    later_cols += [(ZM_COL + n * D_MODEL, ZM_COL + (n + 1) * D_MODEL) for n in range(N_BRANCH)]
    taps_per_group = -(-K_A // len(later_cols))
    z_later = []
    c_a = jnp.broadcast_to(bca_ref[...], (r, D_A))
    for n, (lo, hi) in enumerate(later_cols):
        for k in range(n * taps_per_group, min((n + 1) * taps_per_group, K_A)):
            c_a = c_a + wca_ref[k:k + 1, :] * ua_buf[k * nb:k * nb + r, :]
        z_later.append(zcols(lo, hi))
    z_b, z_g, z_m0, z_m1, z_m2 = z_later
    y_a = _bdot(jax.nn.silu(_layer_norm(c_a, lag_ref[...], lab_ref[...])), wao_ref[...])
    nsa_ref[...] = ua_buf[r:r + ha, :].reshape(K_A - 1, nb, D_A)

    xb_buf[hb:hb + r, :] = z_b
    c_b = jnp.broadcast_to(bcb_ref[...], (r, D_B))
    for k in range(K_B):
        c_b = c_b + wcb_ref[k:k + 1, :] * xb_buf[k * nb:k * nb + r, :]
    nsb_ref[...] = xb_buf[r:r + hb, :].reshape(K_B - 1, nb, D_B)
    c_bf = c_b.astype(BF16)
    ra, rx = [], []
    for h in range(H_B):
        gh = jnp.dot(c_bf[:, h * BW_B:(h + 1) * BW_B], wrg_ref[h], preferred_element_type=F32)
        ra.append(gh[:, :BW_B])
        rx.append(gh[:, BW_B:])
    gate_r = jax.nn.sigmoid(jnp.concatenate(ra, axis=1) + bra_ref[...])
    gate_i = jax.nn.sigmoid(jnp.concatenate(rx, axis=1) + brx_ref[...])
    log_a = (-LRU_C) * gate_r * jax.nn.softplus(-lam_ref[...])
    a_buf[...] = jnp.exp(log_a)
    th = jnp.tanh(log_a)
    u_buf[...] = jnp.sqrt(-2.0 * th / (1.0 - th)) * (gate_i * c_b)
    h = h_carry[...]
    for t in range(tt):
        rows = slice(t * nb, (t + 1) * nb)
        h = a_buf[rows, :] * h + u_buf[rows, :]
        hs_buf[rows, :] = h
    h_carry[...] = h
    hl_ref[...] = h
    y_b = _bdot(hs_buf[...] * jax.nn.gelu(z_g), wbo_ref[...])

    if nt > 1:
        ua_buf[0:ha, :] = ua_buf[r:r + ha, :]
        xb_buf[0:hb, :] = xb_buf[r:r + hb, :]

    y_c = _bdot(_rows_from_seq_major(o_ref, o_buf, nb), wco_ref[...])

    merged = jax.nn.sigmoid(z_m0) * y_a
    merged = merged + jax.nn.sigmoid(z_m1) * y_b
    merged = merged + jax.nn.sigmoid(z_m2) * y_c
    mix = _bdot(merged, wmo_ref[...])
    x1_ref[...] = _layer_norm(ALPHA * x + mix, l1g_ref[...], l1b_ref[...]).reshape(tt, nb, D_MODEL)


def mix(l, o, x, sa_tm, sb_tm, h0, w, w_in_bf, tt, nb, x_bm):
    nbt, t = o.shape[:2]
    nt = t // tt
    assert nt == 1 or tt >= K_A - 1
    r = tt * nb
    grid = (nbt // nb, nt)

    def bm_spec(width):
        return pl.BlockSpec((nb, tt, width), lambda g, i: (g, i, 0))

    def zspec(width, blk):
        return pl.BlockSpec((tt, nb, width), lambda g, i: (i, g, blk))

    def full(a):
        nd = a.ndim
        return pl.BlockSpec(a.shape, lambda g, i: (0,) * nd)

    weights = [w["b_in"], w["w_conv_a"], w["b_conv_a"], w["ln_a_g"], w["ln_a_b"], w["w_a_out"],
               w["w_conv_b"], w["b_conv_b"], w["w_rg"], w["b_rg_a"], w["b_rg_x"], w["lam"],
               w["w_b_out"], w["w_c_out"], w["w_mix_out"], w["ln1_g"], w["ln1_b"]]
    in_specs = ([bm_spec(D_C), bm_spec(D_MODEL) if x_bm else zspec(D_MODEL, 0),
                 pl.BlockSpec((K_A - 1, nb, D_A), lambda g, i: (0, g, 0)),
                 pl.BlockSpec((K_B - 1, nb, D_B), lambda g, i: (0, g, 0)),
                 pl.BlockSpec((nb, D_B), lambda g, i: (g, 0))]
                + [full(a) for a in weights]
                + [pl.BlockSpec(memory_space=pl.ANY)])
    out_specs = [zspec(D_MODEL, 0),
                 pl.BlockSpec((K_A - 1, nb, D_A), lambda g, i: (0, g, 0)),
                 pl.BlockSpec((K_B - 1, nb, D_B), lambda g, i: (0, g, 0)),
                 pl.BlockSpec((nb, D_B), lambda g, i: (g, 0))]
    out_shape = [jax.ShapeDtypeStruct((t, nbt, D_MODEL), F32),
                 jax.ShapeDtypeStruct((K_A - 1, nbt, D_A), F32),
                 jax.ShapeDtypeStruct((K_B - 1, nbt, D_B), F32),
                 jax.ShapeDtypeStruct((nbt, D_B), F32)]
    scratch = [pltpu.VMEM(((K_A - 1) * nb + r, D_A), F32),
               pltpu.VMEM(((K_B - 1) * nb + r, D_B), F32),
               pltpu.VMEM((r, D_B), F32), pltpu.VMEM((r, D_B), F32), pltpu.VMEM((r, D_B), F32),
               pltpu.VMEM((nb, D_B), F32), pltpu.VMEM((D_C // LANES, r, LANES), F32),
               pltpu.VMEM((D_MODEL // LANES, r, LANES), F32),
               pltpu.VMEM((D_MODEL, D_IN), BF16), pltpu.SemaphoreType.DMA(())]
    return pl.pallas_call(
        functools.partial(_mix_kernel, l=l, tt=tt, nb=nb, nt=nt, x_bm=x_bm),
        grid=grid, in_specs=in_specs, out_specs=out_specs, out_shape=out_shape,
        scratch_shapes=scratch,
        compiler_params=_params("arbitrary", "arbitrary"),
        name="mix",
    )(o, x, sa_tm, sb_tm, h0, *weights, w_in_bf)


def _ffn_kernel(x_ref, wg_ref, wu_ref, wd_ref, g_ref, b_ref, o_ref, obm_ref, xb_ref, acc_ref,
                stage_ref, *, nb_seq):
    j = pl.program_id(1)

    @pl.when(j == 0)
    def _():
        xb_ref[...] = x_ref[...].astype(BF16)
        acc_ref[...] = jnp.zeros_like(acc_ref)

    xb = xb_ref[...]
    hg = jnp.dot(xb, wg_ref[...].astype(BF16), preferred_element_type=F32)
    hu = jnp.dot(xb, wu_ref[...].astype(BF16), preferred_element_type=F32)
    acc_ref[...] += _bdot(jax.nn.silu(hg) * hu, wd_ref[...])

    @pl.when(j == pl.num_programs(1) - 1)
    def _():
        res = _layer_norm(ALPHA * x_ref[...] + acc_ref[...], g_ref[...], b_ref[...])
        o_ref[...] = res
        _seq_major_from_rows(obm_ref, stage_ref, res, nb_seq)


def ffn_dense(x, wg, wu, wd, g, b, l, nb_seq):
    m = x.shape[0]
    return pl.pallas_call(
        functools.partial(_ffn_kernel, nb_seq=nb_seq),
        grid=(m // TM, D_FF // TF),
        in_specs=[pl.BlockSpec((TM, D_MODEL), lambda i, j: (i, 0)),
                  pl.BlockSpec((None, D_MODEL, TF), lambda i, j: (l, 0, j)),
                  pl.BlockSpec((None, D_MODEL, TF), lambda i, j: (l, 0, j)),
                  pl.BlockSpec((None, TF, D_MODEL), lambda i, j: (l, j, 0)),
                  pl.BlockSpec((1, D_MODEL), lambda i, j: (0, 0)),
                  pl.BlockSpec((1, D_MODEL), lambda i, j: (0, 0))],
        out_specs=[pl.BlockSpec((TM, D_MODEL), lambda i, j: (i, 0)),
                   pl.BlockSpec((nb_seq, TM // nb_seq, D_MODEL), lambda i, j: (0, i, 0))],
        out_shape=[jax.ShapeDtypeStruct((m, D_MODEL), F32),
                   jax.ShapeDtypeStruct((nb_seq, m // nb_seq, D_MODEL), F32)],
        scratch_shapes=[pltpu.VMEM((TM, D_MODEL), BF16), pltpu.VMEM((TM, D_MODEL), F32),
                        pltpu.VMEM((TOK, TM, LANES), F32)],
        compiler_params=_params("arbitrary", "arbitrary"),
        name="ffn_dense",
    )(x, wg, wu, wd, g.reshape(1, -1), b.reshape(1, -1))


def _router_kernel(x_ref, w_ref, b_ref, sel_ref, prob_ref):
    x = x_ref[...]
    w = w_ref[...]
    xh = x.astype(BF16)
    wh = w.astype(BF16)
    xl = (x - xh.astype(F32)).astype(BF16)
    wl = (w - wh.astype(F32)).astype(BF16)
    dot = functools.partial(jnp.dot, preferred_element_type=F32)
    logits = dot(xh, wh) + (dot(xh, wl) + dot(xl, wh) + dot(xl, wl)) + b_ref[...]
    lane = lax.broadcasted_iota(jnp.int32, logits.shape, 1)
    neg = jnp.float32(-jnp.inf)
    l1 = jnp.where(lane < N_EXPERTS, logits, neg)
    m1 = jnp.max(l1, axis=1, keepdims=True)
    i1 = jnp.min(jnp.where(l1 == m1, lane, LANES), axis=1, keepdims=True)
    l2 = jnp.where(lane == i1, neg, l1)
    m2 = jnp.max(l2, axis=1, keepdims=True)
    i2 = jnp.min(jnp.where(l2 == m2, lane, LANES), axis=1, keepdims=True)
    e2 = jnp.exp(m2 - m1)
    den = 1.0 + e2
    sel_ref[...] = jnp.where(lane == 0, i1, jnp.where(lane == 1, i2, 0))
    prob_ref[...] = jnp.where(lane == 0, 1.0 / den, jnp.where(lane == 1, e2 / den, 0.0))


def router(x, w_router, b_router):
    m = x.shape[0]
    wp = jnp.zeros((D_MODEL, LANES), F32).at[:, :N_EXPERTS].set(w_router)
    bp = jnp.zeros((1, LANES), F32).at[0, :N_EXPERTS].set(b_router)
    return pl.pallas_call(
        _router_kernel,
        grid=(m // TM,),
        in_specs=[pl.BlockSpec((TM, D_MODEL), lambda i: (i, 0)),
                  pl.BlockSpec((D_MODEL, LANES), lambda i: (0, 0)),
                  pl.BlockSpec((1, LANES), lambda i: (0, 0))],
        out_specs=[pl.BlockSpec((TM, LANES), lambda i: (i, 0)),
                   pl.BlockSpec((TM, LANES), lambda i: (i, 0))],
        out_shape=[jax.ShapeDtypeStruct((m, LANES), jnp.int32),
                   jax.ShapeDtypeStruct((m, LANES), F32)],
        compiler_params=_params("arbitrary"),
        name="router",
    )(x, wp, bp)


def _route(sel, tm, n_tiles):
    e = sel[:, :TOP_K].reshape(-1)
    onehot = (e[:, None] == jnp.arange(N_EXPERTS, dtype=jnp.int32)[None, :]).astype(jnp.int32)
    csum = jnp.cumsum(onehot, axis=0)
    rank = jnp.sum(onehot * csum, axis=1) - 1
    counts = csum[-1]
    padded = ((counts + tm - 1) // tm) * tm
    ends = jnp.cumsum(padded)
    starts = ends - padded
    dest = jnp.sum(onehot * starts[None, :], axis=1) + rank
    first_row = jnp.arange(n_tiles, dtype=jnp.int32) * tm
    tile_expert = jnp.sum((first_row[:, None] >= ends[None, :]).astype(jnp.int32), axis=1)
    tile_expert = jnp.minimum(tile_expert, N_EXPERTS - 1)
    n_used = (ends[-1] // tm).reshape(1)
    return dest.astype(jnp.int32), tile_expert.astype(jnp.int32), n_used.astype(jnp.int32)


def _token_copy(src_ref, dst_ref, sem, s_tok, d_tok, n_tok):
    s0 = pl.multiple_of(s_tok * TOK, TOK)
    d0 = pl.multiple_of(d_tok * TOK, TOK)
    return pltpu.make_async_copy(src_ref.at[pl.ds(s0, n_tok * TOK), :],
                                 dst_ref.at[pl.ds(d0, n_tok * TOK), :], sem)


def _to_token_tiles(dst_ref, src, rows):
    for s in range(TOK):
        dst_ref[pl.ds(s, rows, stride=TOK), :] = src[:, s * LANES:(s + 1) * LANES]


def _dispatch_kernel(didx_ref, x_ref, init_ref, out_ref, tok_buf, sem):
    del init_ref
    _to_token_tiles(tok_buf, x_ref[...], TM)

    def issue(t, carry):
        for k in range(TOP_K):
            _token_copy(tok_buf, out_ref, sem, t, didx_ref[0, TOP_K * t + k], 1).start(
                priority=k % N_DMA_PRIORITIES)
        return carry

    lax.fori_loop(0, TM, issue, 0)
    for _ in range(TOP_K):
        _token_copy(tok_buf, out_ref, sem, 0, 0, TM).wait()


def dispatch(x, dest, grouped):
    m = x.shape[0]
    nc = m // TM
    return pl.pallas_call(
        _dispatch_kernel,
        grid=(nc,),
        in_specs=[pl.BlockSpec((None, 1, TOP_K * TM), lambda c: (c, 0, 0),
                               memory_space=pltpu.SMEM),
                  pl.BlockSpec((TM, D_MODEL), lambda c: (c, 0)),
                  pl.BlockSpec(memory_space=pl.ANY)],
        out_specs=pl.BlockSpec(memory_space=pl.ANY),
        out_shape=jax.ShapeDtypeStruct(grouped.shape, grouped.dtype),
        scratch_shapes=[pltpu.VMEM((TM * TOK, LANES), F32), pltpu.SemaphoreType.DMA(())],
        input_output_aliases={2: 0},
        compiler_params=_params("arbitrary"),
        name="dispatch",
    )(dest.reshape(nc, 1, TOP_K * TM), x, grouped)


def _gmm_kernel(te_ref, nu_ref, x_ref, wg_ref, wu_ref, wd_ref, y_ref, xb_ref, acc_ref):
    del te_ref
    i = pl.program_id(0)
    j = pl.program_id(1)

    @pl.when((i >= nu_ref[0]) & (j == 0))
    def _():
        y_ref[...] = jnp.zeros_like(y_ref)

    @pl.when(i < nu_ref[0])
    def _():
        @pl.when(j == 0)
        def _():
            for s in range(TOK):
                xb_ref[:, s * LANES:(s + 1) * LANES] = (
                    x_ref[pl.ds(s, TM, stride=TOK), :].astype(BF16))
            acc_ref[...] = jnp.zeros_like(acc_ref)

        xb = xb_ref[...]
        hg = jnp.dot(xb, wg_ref[...].astype(BF16), preferred_element_type=F32)
        hu = jnp.dot(xb, wu_ref[...].astype(BF16), preferred_element_type=F32)
        acc_ref[...] += _bdot(jax.nn.silu(hg) * hu, wd_ref[...])

        @pl.when(j == pl.num_programs(1) - 1)
        def _():
            _to_token_tiles(y_ref, acc_ref, TM)


def gmm(xs, tile_expert, n_used, wg, wu, wd, l):
    rows = xs.shape[0] // TOK
    n_tiles = rows // TM
    nf = D_FF // TF

    def tile(i, nu):
        return jnp.minimum(i, nu[0] - 1)

    def ff(i, j, nu):
        return jnp.where(i < nu[0], j, nf - 1)

    grid_spec = pltpu.PrefetchScalarGridSpec(
        num_scalar_prefetch=2,
        grid=(n_tiles, nf),
        in_specs=[pl.BlockSpec((TM * TOK, LANES), lambda i, j, te, nu: (i, 0)),
                  pl.BlockSpec((None, None, D_MODEL, TF),
                               lambda i, j, te, nu: (l, te[tile(i, nu)], 0, ff(i, j, nu))),
                  pl.BlockSpec((None, None, D_MODEL, TF),
                               lambda i, j, te, nu: (l, te[tile(i, nu)], 0, ff(i, j, nu))),
                  pl.BlockSpec((None, None, TF, D_MODEL),
                               lambda i, j, te, nu: (l, te[tile(i, nu)], ff(i, j, nu), 0))],
        out_specs=pl.BlockSpec((TM * TOK, LANES), lambda i, j, te, nu: (i, 0)),
        scratch_shapes=[pltpu.VMEM((TM, D_MODEL), BF16), pltpu.VMEM((TM, D_MODEL), F32)],
    )
    return pl.pallas_call(
        _gmm_kernel,
        grid_spec=grid_spec,
        out_shape=jax.ShapeDtypeStruct(xs.shape, F32),
        compiler_params=_params("arbitrary", "arbitrary"),
        name="gmm",
    )(tile_expert, n_used, xs, wg, wu, wd)


def _combine_kernel(didx_ref, x_ref, prob_ref, g_ref, b_ref, y_ref, o_ref, ybuf, stage_ref, sem,
                    *, nb_seq):
    def issue(t, carry):
        for k in range(TOP_K):
            _token_copy(y_ref, ybuf, sem, didx_ref[0, TOP_K * t + k], k * TM + t, 1).start(
                priority=k % N_DMA_PRIORITIES)
        return carry

    lax.fori_loop(0, TM, issue, 0)
    for k in range(TOP_K):
        _token_copy(y_ref, ybuf, sem, 0, k * TM, TM).wait()
    p0 = prob_ref[:, 0:1]
    p1 = prob_ref[:, 1:2]
    f = jnp.concatenate(
        [p0 * ybuf[pl.ds(s, TM, stride=TOK), :] + p1 * ybuf[pl.ds(TM * TOK + s, TM, stride=TOK), :]
         for s in range(TOK)], axis=1)
    res = _layer_norm(ALPHA * x_ref[...] + f, g_ref[...], b_ref[...])
    _seq_major_from_rows(o_ref, stage_ref, res, nb_seq)


def combine(x, y, dest, prob, g, b, nb_seq):
    m = x.shape[0]
    nc = m // TM
    return pl.pallas_call(
        functools.partial(_combine_kernel, nb_seq=nb_seq),
        grid=(nc,),
        in_specs=[pl.BlockSpec((None, 1, TOP_K * TM), lambda i: (i, 0, 0),
                               memory_space=pltpu.SMEM),
                  pl.BlockSpec((TM, D_MODEL), lambda i: (i, 0)),
                  pl.BlockSpec((TM, LANES), lambda i: (i, 0)),
                  pl.BlockSpec((1, D_MODEL), lambda i: (0, 0)),
                  pl.BlockSpec((1, D_MODEL), lambda i: (0, 0)),
                  pl.BlockSpec(memory_space=pl.ANY)],
        out_specs=pl.BlockSpec((nb_seq, TM // nb_seq, D_MODEL), lambda i: (0, i, 0)),
        out_shape=jax.ShapeDtypeStruct((nb_seq, m // nb_seq, D_MODEL), F32),
        scratch_shapes=[pltpu.VMEM((TOP_K * TM * TOK, LANES), F32),
                        pltpu.VMEM((TOK, TM, LANES), F32), pltpu.SemaphoreType.DMA(())],
        compiler_params=_params("arbitrary"),
        name="combine",
    )(dest.reshape(nc, 1, TOP_K * TM), x, prob, g.reshape(1, -1), b.reshape(1, -1), y)


def moe_routed(xs_list, nb_seqs, p, l):
    j = l // 2
    routed = [router(x, p["w_router"][j], p["b_router"][j]) for x in xs_list]
    sel = jnp.concatenate([r[0] for r in routed], axis=0)
    n_pairs = TOP_K * sel.shape[0]
    n_tiles = n_pairs // TM + N_EXPERTS
    dest, tile_expert, n_used = _route(sel, TM, n_tiles)
    bounds = [0]
    for x in xs_list:
        bounds.append(bounds[-1] + TOP_K * x.shape[0])

    grouped = jnp.zeros((n_tiles * TM * TOK, LANES), F32)
    for x, lo, hi in zip(xs_list, bounds[:-1], bounds[1:]):
        grouped = dispatch(x, dest[lo:hi], grouped)
    y = gmm(grouped, tile_expert, n_used, p["w_e_gate"], p["w_e_up"], p["w_e_down"], j)
    return [combine(x, y, dest[lo:hi], prob, p["ln2_g"][l], p["ln2_b"][l], nb)
            for x, nb, (_, prob), lo, hi in zip(xs_list, nb_seqs, routed, bounds[:-1], bounds[1:])]


def _layer_weights(l, p):
    row = lambda a: a[l].reshape(1, -1)
    return {
        "b_in": row(p["b_in"]),
        "w_conv_a": p["w_conv_a"][l], "b_conv_a": row(p["b_conv_a"]),
        "ln_a_g": row(p["ln_a_g"]), "ln_a_b": row(p["ln_a_b"]),
        "w_a_out": p["w_a_out"][l].astype(BF16),
        "w_conv_b": p["w_conv_b"][l], "b_conv_b": row(p["b_conv_b"]),
        "w_rg": jnp.concatenate([p["w_rg_a"][l], p["w_rg_x"][l]], axis=-1).astype(BF16),
        "b_rg_a": row(p["b_rg_a"]), "b_rg_x": row(p["b_rg_x"]), "lam": row(p["lru_lambda"]),
        "w_b_out": p["w_b_out"][l].astype(BF16), "w_c_out": p["w_c_out"][l].astype(BF16),
        "w_mix_out": p["w_mix_out"][l].astype(BF16),
        "ln1_g": row(p["ln1_g"]), "ln1_b": row(p["ln1_b"]),
    }


def _mix_layer(l, w, w_in_bf, x, o, sa_tm, sb_tm, h0, tt, nb, x_bm):
    nbt, t = o.shape[:2]
    x1, nsa, nsb, hl = mix(l, o, x, sa_tm, sb_tm, h0, w, w_in_bf, tt, nb, x_bm)
    return x1.reshape(t * nbt, D_MODEL), nsa, nsb, hl


def _to_tm(a):
    return jnp.transpose(a, (1, 0, 2))


def kernel(x_prompt, x_sample, state_conv_a, state_conv_b, state_rglru, cache_mem_k, cache_mem_v, mem_prompt, w_in, b_in, w_conv_a, b_conv_a, ln_a_g, ln_a_b, w_a_out, w_conv_b, b_conv_b, w_rg_a, b_rg_a, w_rg_x, b_rg_x, lru_lambda, w_b_out, w_mem_kv, w_c_out, w_mix_out, ln1_g, ln1_b, w_ff_gate, w_ff_up, w_ff_down, w_router, b_router, w_e_gate, w_e_up, w_e_down, ln2_g, ln2_b):
    p = dict(w_in=w_in, b_in=b_in, w_conv_a=w_conv_a, b_conv_a=b_conv_a, ln_a_g=ln_a_g,
             ln_a_b=ln_a_b, w_a_out=w_a_out, w_conv_b=w_conv_b, b_conv_b=b_conv_b,
             w_rg_a=w_rg_a, b_rg_a=b_rg_a, w_rg_x=w_rg_x, b_rg_x=b_rg_x, lru_lambda=lru_lambda,
             w_b_out=w_b_out, w_c_out=w_c_out, w_mix_out=w_mix_out, ln1_g=ln1_g, ln1_b=ln1_b,
             w_ff_gate=w_ff_gate, w_ff_up=w_ff_up, w_ff_down=w_ff_down, w_router=w_router,
             b_router=b_router, w_e_gate=w_e_gate, w_e_up=w_e_up, w_e_down=w_e_down,
             ln2_g=ln2_g, ln2_b=ln2_b)
    bp, tp, _ = x_prompt.shape
    bs, ts, _ = x_sample.shape
    n_mem = mem_prompt.shape[1]

    xp_bm, xs_bm = x_prompt, x_sample
    xp_tm = xs_tm = None
    w_in_bf = w_in.astype(BF16)
    b_in3 = b_in.reshape(DEPTH, 1, D_IN)
    zero_a = jnp.zeros((K_A - 1, bp, D_A), F32)
    zero_b = jnp.zeros((K_B - 1, bp, D_B), F32)
    zero_h = jnp.zeros((bp, D_B), F32)
    zero_bias = jnp.zeros((DEPTH, 1, 2 * D_C), F32)
    cache_k = cache_mem_k.reshape(DEPTH, bs, n_mem * H_C, DH_C)
    cache_v = cache_mem_v.reshape(DEPTH, bs, n_mem * H_C, DH_C)
    outs = {k: [] for k in ("pa", "pb", "ph", "pk", "pv", "sa", "sb", "sh")}
    for l in range(DEPTH):
        w = _layer_weights(l, p)
        kv = matmul_bias(mem_prompt.reshape(bp * n_mem, D_MODEL), w_mem_kv, zero_bias, l,
                         Z_BLOCK, "kv_proj").reshape(bp, n_mem, 2 * D_C)
        mk = kv[..., :D_C]
        mv = kv[..., D_C:]
        kv4 = kv.reshape(1, bp, n_mem, 2 * D_C)
        op = attention(xp_bm, w_in_bf, b_in3, l, kv4, kv4, (0, 0, 1), tq=512)
        xp1, na, nb_, nh = _mix_layer(l, w, w_in_bf, xp_bm if xp_tm is None else xp_tm, op,
                                      zero_a, zero_b, zero_h, tt=32, nb=bp, x_bm=xp_tm is None)
        outs["pa"].append(_to_tm(na))
        outs["pb"].append(_to_tm(nb_))
        outs["ph"].append(nh)
        outs["pk"].append(mk.reshape(bp, n_mem, H_C, DH_C))
        outs["pv"].append(mv.reshape(bp, n_mem, H_C, DH_C))
        os_ = attention_rows(xs_bm, w_in_bf, b_in3, l, cache_k, cache_v, sb=8)
        xs1, na, nb_, nh = _mix_layer(l, w, w_in_bf, xs_bm if xs_tm is None else xs_tm, os_,
                                      _to_tm(state_conv_a[l]), _to_tm(state_conv_b[l]),
                                      state_rglru[l], tt=ts, nb=32, x_bm=xs_tm is None)
        outs["sa"].append(_to_tm(na))
        outs["sb"].append(_to_tm(nb_))
        outs["sh"].append(nh)
        if l % 2 == 0:
            j = l // 2
            (xp_rows, xp_bm), (xs_rows, xs_bm) = [
                ffn_dense(x, w_ff_gate, w_ff_up, w_ff_down, ln2_g[l], ln2_b[l], j, nb)
                for x, nb in ((xp1, bp), (xs1, bs))]
            xp_tm = xp_rows.reshape(tp, bp, D_MODEL)
            xs_tm = xs_rows.reshape(ts, bs, D_MODEL)
        else:
            xp_bm, xs_bm = moe_routed([xp1, xs1], [bp, bs], p, l)
            xp_tm = xs_tm = None

    st = lambda k: jnp.stack(outs[k])
    return (xp_bm, xs_bm, st("pa"), st("pb"), st("ph"), st("pk"), st("pv"),
            st("sa"), st("sb"), st("sh"))
```

```python
import functools

import jax
import jax.numpy as jnp
from jax import lax
from jax.experimental import pallas as pl
from jax.experimental.pallas import tpu as pltpu

D_MODEL = 1024
N_MEM = 256
D_A = 512
K_A = 31
D_B = 1024
K_B = 4
H_B = 8
BW_B = D_B // H_B
LRU_C = 8.0
D_C = 512
H_C = 4
DH_C = D_C // H_C
N_BRANCH = 3
D_IN = 2 * D_A + 2 * D_B + D_C + N_BRANCH * D_MODEL
D_FF = 2816
N_EXPERTS = 8
TOP_K = 2
DEPTH = 2
ALPHA = (2.0 * DEPTH) ** 0.25
LN_EPS = 1e-5

Z_BLOCK = 512
Q_BLOCK = (2 * D_A + 2 * D_B) // Z_BLOCK
ZM_COL = 2 * D_A + 2 * D_B + D_C

LANES = 128
VMEM_LIMIT = 56 * 1024 * 1024

TM = 1024
TF = 256
N_DMA_PRIORITIES = 2
ISSUE_UNROLL = 8
TOK = D_MODEL // LANES
assert TOK == 8

BF16 = jnp.bfloat16
F32 = jnp.float32


def _params(*sem):
    return pltpu.CompilerParams(dimension_semantics=sem, vmem_limit_bytes=VMEM_LIMIT)


def _layer_norm(x, g, b):
    mu = jnp.mean(x, axis=-1, keepdims=True)
    xc = x - mu
    var = jnp.mean(xc * xc, axis=-1, keepdims=True)
    return xc * lax.rsqrt(var + LN_EPS) * g + b


def _bdot(a, b):
    return jnp.dot(a.astype(BF16), b.astype(BF16), preferred_element_type=F32)


def _rows_from_seq_major(src_ref, stage_ref, nb):
    _, tt, c = src_ref.shape
    for s in range(nb):
        for k in range(c // LANES):
            stage_ref[k, pl.ds(s, tt, stride=nb), :] = src_ref[s, :, k * LANES:(k + 1) * LANES]
    return jnp.concatenate([stage_ref[k] for k in range(c // LANES)], axis=1)


def _seq_major_from_rows(dst_ref, stage_ref, rows, nb):
    r, c = rows.shape
    for k in range(c // LANES):
        stage_ref[k] = rows[:, k * LANES:(k + 1) * LANES]
    for s in range(nb):
        for k in range(c // LANES):
            dst_ref[s, :, k * LANES:(k + 1) * LANES] = stage_ref[k, pl.ds(s, r // nb, stride=nb), :]


def _matmul_bias_kernel(x_ref, w_ref, b_ref, o_ref, xb_ref):
    @pl.when(pl.program_id(1) == 0)
    def _():
        xb_ref[...] = x_ref[...].astype(BF16)

    o_ref[...] = jnp.dot(xb_ref[...], w_ref[...].astype(BF16),
                         preferred_element_type=F32) + b_ref[...]


def matmul_bias(x, w, b, l, tn, name):
    m, k = x.shape
    n = w.shape[2]
    return pl.pallas_call(
        _matmul_bias_kernel,
        grid=(m // TM, n // tn),
        in_specs=[pl.BlockSpec((TM, k), lambda i, j: (i, 0)),
                  pl.BlockSpec((None, k, tn), lambda i, j: (l, 0, j)),
                  pl.BlockSpec((None, 1, tn), lambda i, j: (l, 0, j))],
        out_specs=pl.BlockSpec((TM, tn), lambda i, j: (i, j)),
        out_shape=jax.ShapeDtypeStruct((m, n), F32),
        scratch_shapes=[pltpu.VMEM((TM, k), BF16)],
        compiler_params=_params("arbitrary", "arbitrary"),
        name=name,
    )(x, w, b)


def _attention_kernel(x_ref, wq_ref, bq_ref, k_ref, v_ref, o_ref):
    q = jnp.dot(x_ref[...].astype(BF16), wq_ref[...], preferred_element_type=F32) + bq_ref[...]
    for h in range(H_C):
        cols = slice(h * DH_C, (h + 1) * DH_C)
        kh = k_ref[:, cols].astype(BF16)
        vh = v_ref[:, cols].astype(BF16)
        sc = lax.dot_general(q[:, cols].astype(BF16), kh, (((1,), (1,)), ((), ())),
                             preferred_element_type=F32) * (DH_C ** -0.5)
        e = jnp.exp(sc - jnp.max(sc, axis=-1, keepdims=True))
        p = e * (1.0 / jnp.sum(e, axis=-1, keepdims=True))
        o_ref[:, cols] = jnp.dot(p.astype(BF16), vh, preferred_element_type=F32)


def attention(x, w_in_bf, b_in, l, k, v, kv_index, tq):
    s, t, _ = x.shape
    lk, kc, vc = kv_index
    return pl.pallas_call(
        _attention_kernel,
        grid=(s, t // tq),
        in_specs=[pl.BlockSpec((None, tq, D_MODEL), lambda i, j: (i, j, 0)),
                  pl.BlockSpec((None, D_MODEL, D_C), lambda i, j: (l, 0, Q_BLOCK)),
                  pl.BlockSpec((None, 1, D_C), lambda i, j: (l, 0, Q_BLOCK)),
                  pl.BlockSpec((None, None, N_MEM, D_C), lambda i, j: (lk, i, 0, kc)),
                  pl.BlockSpec((None, None, N_MEM, D_C), lambda i, j: (lk, i, 0, vc))],
        out_specs=pl.BlockSpec((None, tq, D_C), lambda i, j: (i, j, 0)),
        out_shape=jax.ShapeDtypeStruct((s, t, D_C), F32),
        compiler_params=_params("arbitrary", "arbitrary"),
        name="attention",
    )(x, w_in_bf, b_in, k, v)


def _attention_rows_kernel(x_ref, wq_ref, bq_ref, k_ref, v_ref, o_ref, *, sb, t):
    n_rows = H_C * t
    n_cols = N_MEM * H_C
    row_head = lax.broadcasted_iota(jnp.int32, (n_rows, n_cols), 0) // t
    col_head = lax.broadcasted_iota(jnp.int32, (n_rows, n_cols), 1) % H_C
    own = row_head == col_head
    qs = (jnp.dot(x_ref[...].reshape(sb * t, D_MODEL).astype(BF16), wq_ref[...],
                  preferred_element_type=F32) + bq_ref[...])
    for s in range(sb):
        q = qs[s * t:(s + 1) * t, :]
        qa = jnp.concatenate([q[:, h * DH_C:(h + 1) * DH_C] for h in range(H_C)], axis=0)
        sc = lax.dot_general(qa.astype(BF16), k_ref[s].astype(BF16), (((1,), (1,)), ((), ())),
                             preferred_element_type=F32) * (DH_C ** -0.5)
        sc = jnp.where(own, sc, -jnp.inf)
        e = jnp.exp(sc - jnp.max(sc, axis=-1, keepdims=True))
        p = e * (1.0 / jnp.sum(e, axis=-1, keepdims=True))
        oa = jnp.dot(p.astype(BF16), v_ref[s].astype(BF16), preferred_element_type=F32)
        o_ref[s] = jnp.concatenate([oa[h * t:(h + 1) * t, :] for h in range(H_C)], axis=1)


def attention_rows(x, w_in_bf, b_in, l, k, v, sb):
    s, t, _ = x.shape
    rows = N_MEM * H_C
    return pl.pallas_call(
        functools.partial(_attention_rows_kernel, sb=sb, t=t),
        grid=(s // sb,),
        in_specs=[pl.BlockSpec((sb, t, D_MODEL), lambda i: (i, 0, 0)),
                  pl.BlockSpec((None, D_MODEL, D_C), lambda i: (l, 0, Q_BLOCK)),
                  pl.BlockSpec((None, 1, D_C), lambda i: (l, 0, Q_BLOCK)),
                  pl.BlockSpec((None, sb, rows, DH_C), lambda i: (l, i, 0, 0)),
                  pl.BlockSpec((None, sb, rows, DH_C), lambda i: (l, i, 0, 0))],
        out_specs=pl.BlockSpec((sb, t, D_C), lambda i: (i, 0, 0)),
        out_shape=jax.ShapeDtypeStruct((s, t, D_C), F32),
        compiler_params=_params("arbitrary"),
        name="attention_rows",
    )(x, w_in_bf, b_in, k, v)


def _mix_kernel(o_ref, x_ref, sa_ref, sb_ref, h0_ref, bin_ref,
                wca_ref, bca_ref, lag_ref, lab_ref, wao_ref,
                wcb_ref, bcb_ref, wrg_ref, bra_ref, brx_ref, lam_ref, wbo_ref,
                wco_ref, wmo_ref, l1g_ref, l1b_ref, win_hbm,
                x1_ref, nsa_ref, nsb_ref, hl_ref,
                ua_buf, xb_buf, a_buf, u_buf, hs_buf, h_carry, o_buf, x_buf, win_buf, win_sem,
                *, l, tt, nb, nt, x_bm):
    r = tt * nb
    ha = (K_A - 1) * nb
    hb = (K_B - 1) * nb
    i = pl.program_id(1)

    @pl.when((pl.program_id(0) == 0) & (i == 0))
    def _():
        load = pltpu.make_async_copy(win_hbm.at[l], win_buf, win_sem)
        load.start()
        load.wait()

    @pl.when(i == 0)
    def _():
        ua_buf[0:ha, :] = sa_ref[...].reshape(ha, D_A)
        xb_buf[0:hb, :] = sb_ref[...].reshape(hb, D_B)
        h_carry[...] = h0_ref[...]

    if x_bm:
        x = _rows_from_seq_major(x_ref, x_buf, nb)
    else:
        x = x_ref[...].reshape(r, D_MODEL)
    xb = x.astype(BF16)

    def zcols(lo, hi):
        return (jnp.dot(xb, win_buf[:, lo:hi], preferred_element_type=F32) + bin_ref[:, lo:hi])

    za = zcols(0, 2 * D_A)
    ua_buf[ha:ha + r, :] = za[:, :D_A] * jax.nn.sigmoid(za[:, D_A:])
    later_cols = [(2 * D_A + n * D_B, 2 * D_A + (n + 1) * D_B) for n in range(2)]
    later_cols += [(ZM_COL + n * D_MODEL, ZM_COL + (n + 1) * D_MODEL) for n in range(N_BRANCH)]
    taps_per_group = -(-K_A // len(later_cols))
    z_later = []
    c_a = jnp.broadcast_to(bca_ref[...], (r, D_A))
    for n, (lo, hi) in enumerate(later_cols):
        for k in range(n * taps_per_group, min((n + 1) * taps_per_group, K_A)):
            c_a = c_a + wca_ref[k:k + 1, :] * ua_buf[k * nb:k * nb + r, :]
        z_later.append(zcols(lo, hi))
    z_b, z_g, z_m0, z_m1, z_m2 = z_later
    y_a = _bdot(jax.nn.silu(_layer_norm(c_a, lag_ref[...], lab_ref[...])), wao_ref[...])
    nsa_ref[...] = ua_buf[r:r + ha, :].reshape(K_A - 1, nb, D_A)

    xb_buf[hb:hb + r, :] = z_b
    c_b = jnp.broadcast_to(bcb_ref[...], (r, D_B))
    for k in range(K_B):
        c_b = c_b + wcb_ref[k:k + 1, :] * xb_buf[k * nb:k * nb + r, :]
    nsb_ref[...] = xb_buf[r:r + hb, :].reshape(K_B - 1, nb, D_B)
    c_bf = c_b.astype(BF16)
    ra, rx = [], []
    for h in range(H_B):
        gh = jnp.dot(c_bf[:, h * BW_B:(h + 1) * BW_B], wrg_ref[h], preferred_element_type=F32)
        ra.append(gh[:, :BW_B])
        rx.append(gh[:, BW_B:])
    gate_r = jax.nn.sigmoid(jnp.concatenate(ra, axis=1) + bra_ref[...])
    gate_i = jax.nn.sigmoid(jnp.concatenate(rx, axis=1) + brx_ref[...])
    log_a = (-LRU_C) * gate_r * jax.nn.softplus(-lam_ref[...])
    a_buf[...] = jnp.exp(log_a)
    th = jnp.tanh(log_a)
    u_buf[...] = jnp.sqrt(-2.0 * th / (1.0 - th)) * (gate_i * c_b)
    h = h_carry[...]
    for t in range(tt):
        rows = slice(t * nb, (t + 1) * nb)
        h = a_buf[rows, :] * h + u_buf[rows, :]
        hs_buf[rows, :] = h
    h_carry[...] = h
    hl_ref[...] = h
    y_b = _bdot(hs_buf[...] * jax.nn.gelu(z_g), wbo_ref[...])

    if nt > 1:
        ua_buf[0:ha, :] = ua_buf[r:r + ha, :]
        xb_buf[0:hb, :] = xb_buf[r:r + hb, :]

    y_c = _bdot(_rows_from_seq_major(o_ref, o_buf, nb), wco_ref[...])

    merged = jax.nn.sigmoid(z_m0) * y_a
    merged = merged + jax.nn.sigmoid(z_m1) * y_b
    merged = merged + jax.nn.sigmoid(z_m2) * y_c
    mix = _bdot(merged, wmo_ref[...])
    x1_ref[...] = _layer_norm(ALPHA * x + mix, l1g_ref[...], l1b_ref[...]).reshape(tt, nb, D_MODEL)


def mix(l, o, x, sa_tm, sb_tm, h0, w, w_in_bf, tt, nb, x_bm):
    nbt, t = o.shape[:2]
    nt = t // tt
    assert nt == 1 or tt >= K_A - 1
    r = tt * nb
    grid = (nbt // nb, nt)

    def bm_spec(width):
        return pl.BlockSpec((nb, tt, width), lambda g, i: (g, i, 0))

    def zspec(width, blk):
        return pl.BlockSpec((tt, nb, width), lambda g, i: (i, g, blk))

    def full(a):
        nd = a.ndim
        return pl.BlockSpec(a.shape, lambda g, i: (0,) * nd)

    weights = [w["b_in"], w["w_conv_a"], w["b_conv_a"], w["ln_a_g"], w["ln_a_b"], w["w_a_out"],
               w["w_conv_b"], w["b_conv_b"], w["w_rg"], w["b_rg_a"], w["b_rg_x"], w["lam"],
               w["w_b_out"], w["w_c_out"], w["w_mix_out"], w["ln1_g"], w["ln1_b"]]
    in_specs = ([bm_spec(D_C), bm_spec(D_MODEL) if x_bm else zspec(D_MODEL, 0),
                 pl.BlockSpec((K_A - 1, nb, D_A), lambda g, i: (0, g, 0)),
                 pl.BlockSpec((K_B - 1, nb, D_B), lambda g, i: (0, g, 0)),
                 pl.BlockSpec((nb, D_B), lambda g, i: (g, 0))]
                + [full(a) for a in weights]
                + [pl.BlockSpec(memory_space=pl.ANY)])
    out_specs = [zspec(D_MODEL, 0),
                 pl.BlockSpec((K_A - 1, nb, D_A), lambda g, i: (0, g, 0)),
                 pl.BlockSpec((K_B - 1, nb, D_B), lambda g, i: (0, g, 0)),
                 pl.BlockSpec((nb, D_B), lambda g, i: (g, 0))]
    out_shape = [jax.ShapeDtypeStruct((t, nbt, D_MODEL), F32),
                 jax.ShapeDtypeStruct((K_A - 1, nbt, D_A), F32),
                 jax.ShapeDtypeStruct((K_B - 1, nbt, D_B), F32),
                 jax.ShapeDtypeStruct((nbt, D_B), F32)]
    scratch = [pltpu.VMEM(((K_A - 1) * nb + r, D_A), F32),
               pltpu.VMEM(((K_B - 1) * nb + r, D_B), F32),
               pltpu.VMEM((r, D_B), F32), pltpu.VMEM((r, D_B), F32), pltpu.VMEM((r, D_B), F32),
               pltpu.VMEM((nb, D_B), F32), pltpu.VMEM((D_C // LANES, r, LANES), F32),
               pltpu.VMEM((D_MODEL // LANES, r, LANES), F32),
               pltpu.VMEM((D_MODEL, D_IN), BF16), pltpu.SemaphoreType.DMA(())]
    return pl.pallas_call(
        functools.partial(_mix_kernel, l=l, tt=tt, nb=nb, nt=nt, x_bm=x_bm),
        grid=grid, in_specs=in_specs, out_specs=out_specs, out_shape=out_shape,
        scratch_shapes=scratch,
        compiler_params=_params("arbitrary", "arbitrary"),
        name="mix",
    )(o, x, sa_tm, sb_tm, h0, *weights, w_in_bf)


def _ffn_kernel(x_ref, wg_ref, wu_ref, wd_ref, g_ref, b_ref, o_ref, obm_ref, xb_ref, acc_ref,
                stage_ref, *, nb_seq):
    j = pl.program_id(1)

    @pl.when(j == 0)
    def _():
        xb_ref[...] = x_ref[...].astype(BF16)
        acc_ref[...] = jnp.zeros_like(acc_ref)

    xb = xb_ref[...]
    hg = jnp.dot(xb, wg_ref[...].astype(BF16), preferred_element_type=F32)
    hu = jnp.dot(xb, wu_ref[...].astype(BF16), preferred_element_type=F32)
    acc_ref[...] += _bdot(jax.nn.silu(hg) * hu, wd_ref[...])

    @pl.when(j == pl.num_programs(1) - 1)
    def _():
        res = _layer_norm(ALPHA * x_ref[...] + acc_ref[...], g_ref[...], b_ref[...])
        o_ref[...] = res
        _seq_major_from_rows(obm_ref, stage_ref, res, nb_seq)


def ffn_dense(x, wg, wu, wd, g, b, l, nb_seq):
    m = x.shape[0]
    return pl.pallas_call(
        functools.partial(_ffn_kernel, nb_seq=nb_seq),
        grid=(m // TM, D_FF // TF),
        in_specs=[pl.BlockSpec((TM, D_MODEL), lambda i, j: (i, 0)),
                  pl.BlockSpec((None, D_MODEL, TF), lambda i, j: (l, 0, j)),
                  pl.BlockSpec((None, D_MODEL, TF), lambda i, j: (l, 0, j)),
                  pl.BlockSpec((None, TF, D_MODEL), lambda i, j: (l, j, 0)),
                  pl.BlockSpec((1, D_MODEL), lambda i, j: (0, 0)),
                  pl.BlockSpec((1, D_MODEL), lambda i, j: (0, 0))],
        out_specs=[pl.BlockSpec((TM, D_MODEL), lambda i, j: (i, 0)),
                   pl.BlockSpec((nb_seq, TM // nb_seq, D_MODEL), lambda i, j: (0, i, 0))],
        out_shape=[jax.ShapeDtypeStruct((m, D_MODEL), F32),
                   jax.ShapeDtypeStruct((nb_seq, m // nb_seq, D_MODEL), F32)],
        scratch_shapes=[pltpu.VMEM((TM, D_MODEL), BF16), pltpu.VMEM((TM, D_MODEL), F32),
                        pltpu.VMEM((TOK, TM, LANES), F32)],
        compiler_params=_params("arbitrary", "arbitrary"),
        name="ffn_dense",
    )(x, wg, wu, wd, g.reshape(1, -1), b.reshape(1, -1))


def _router_kernel(x_ref, w_ref, b_ref, sel_ref, prob_ref):
    x = x_ref[...]
    w = w_ref[...]
    xh = x.astype(BF16)
    wh = w.astype(BF16)
    xl = (x - xh.astype(F32)).astype(BF16)
    wl = (w - wh.astype(F32)).astype(BF16)
    dot = functools.partial(jnp.dot, preferred_element_type=F32)
    logits = dot(xh, wh) + (dot(xh, wl) + dot(xl, wh) + dot(xl, wl)) + b_ref[...]
    lane = lax.broadcasted_iota(jnp.int32, logits.shape, 1)
    neg = jnp.float32(-jnp.inf)
    l1 = jnp.where(lane < N_EXPERTS, logits, neg)
    m1 = jnp.max(l1, axis=1, keepdims=True)
    i1 = jnp.min(jnp.where(l1 == m1, lane, LANES), axis=1, keepdims=True)
    l2 = jnp.where(lane == i1, neg, l1)
    m2 = jnp.max(l2, axis=1, keepdims=True)
    i2 = jnp.min(jnp.where(l2 == m2, lane, LANES), axis=1, keepdims=True)
    e2 = jnp.exp(m2 - m1)
    den = 1.0 + e2
    sel_ref[...] = jnp.where(lane == 0, i1, jnp.where(lane == 1, i2, 0))
    prob_ref[...] = jnp.where(lane == 0, 1.0 / den, jnp.where(lane == 1, e2 / den, 0.0))


def router(x, w_router, b_router):
    m = x.shape[0]
    wp = jnp.zeros((D_MODEL, LANES), F32).at[:, :N_EXPERTS].set(w_router)
    bp = jnp.zeros((1, LANES), F32).at[0, :N_EXPERTS].set(b_router)
    return pl.pallas_call(
        _router_kernel,
        grid=(m // TM,),
        in_specs=[pl.BlockSpec((TM, D_MODEL), lambda i: (i, 0)),
                  pl.BlockSpec((D_MODEL, LANES), lambda i: (0, 0)),
                  pl.BlockSpec((1, LANES), lambda i: (0, 0))],
        out_specs=[pl.BlockSpec((TM, LANES), lambda i: (i, 0)),
                   pl.BlockSpec((TM, LANES), lambda i: (i, 0))],
        out_shape=[jax.ShapeDtypeStruct((m, LANES), jnp.int32),
                   jax.ShapeDtypeStruct((m, LANES), F32)],
        compiler_params=_params("arbitrary"),
        name="router",
    )(x, wp, bp)


def _route(sel, tm, n_tiles):
    e = sel[:, :TOP_K].reshape(-1)
    onehot = (e[:, None] == jnp.arange(N_EXPERTS, dtype=jnp.int32)[None, :]).astype(jnp.int32)
    csum = jnp.cumsum(onehot, axis=0)
    rank = jnp.sum(onehot * csum, axis=1) - 1
    counts = csum[-1]
    padded = ((counts + tm - 1) // tm) * tm
    ends = jnp.cumsum(padded)
    starts = ends - padded
    dest = jnp.sum(onehot * starts[None, :], axis=1) + rank
    first_row = jnp.arange(n_tiles, dtype=jnp.int32) * tm
    tile_expert = jnp.sum((first_row[:, None] >= ends[None, :]).astype(jnp.int32), axis=1)
    tile_expert = jnp.minimum(tile_expert, N_EXPERTS - 1)
    n_used = (ends[-1] // tm).reshape(1)
    return dest.astype(jnp.int32), tile_expert.astype(jnp.int32), n_used.astype(jnp.int32)


def _token_copy(src_ref, dst_ref, sem, s_tok, d_tok, n_tok):
    s0 = pl.multiple_of(s_tok * TOK, TOK)
    d0 = pl.multiple_of(d_tok * TOK, TOK)
    return pltpu.make_async_copy(src_ref.at[pl.ds(s0, n_tok * TOK), :],
                                 dst_ref.at[pl.ds(d0, n_tok * TOK), :], sem)


def _to_token_tiles(dst_ref, src, rows):
    for s in range(TOK):
        dst_ref[pl.ds(s, rows, stride=TOK), :] = src[:, s * LANES:(s + 1) * LANES]


def _dispatch_kernel(didx_ref, x_ref, init_ref, out_ref, tok_buf, sem):
    del init_ref
    _to_token_tiles(tok_buf, x_ref[...], TM)

    def issue(t, carry):
        for k in range(TOP_K):
            _token_copy(tok_buf, out_ref, sem, t, didx_ref[0, TOP_K * t + k], 1).start(
                priority=k % N_DMA_PRIORITIES)
        return carry

    lax.fori_loop(0, TM, issue, 0)
    for _ in range(TOP_K):
        _token_copy(tok_buf, out_ref, sem, 0, 0, TM).wait()


def dispatch(x, dest, grouped):
    m = x.shape[0]
    nc = m // TM
    return pl.pallas_call(
        _dispatch_kernel,
        grid=(nc,),
        in_specs=[pl.BlockSpec((None, 1, TOP_K * TM), lambda c: (c, 0, 0),
                               memory_space=pltpu.SMEM),
                  pl.BlockSpec((TM, D_MODEL), lambda c: (c, 0)),
                  pl.BlockSpec(memory_space=pl.ANY)],
        out_specs=pl.BlockSpec(memory_space=pl.ANY),
        out_shape=jax.ShapeDtypeStruct(grouped.shape, grouped.dtype),
        scratch_shapes=[pltpu.VMEM((TM * TOK, LANES), F32), pltpu.SemaphoreType.DMA(())],
        input_output_aliases={2: 0},
        compiler_params=_params("arbitrary"),
        name="dispatch",
    )(dest.reshape(nc, 1, TOP_K * TM), x, grouped)


def _gmm_kernel(te_ref, nu_ref, x_ref, wg_ref, wu_ref, wd_ref, y_ref, xb_ref, acc_ref):
    del te_ref
    i = pl.program_id(0)
    j = pl.program_id(1)

    @pl.when((i >= nu_ref[0]) & (j == 0))
    def _():
        y_ref[...] = jnp.zeros_like(y_ref)

    @pl.when(i < nu_ref[0])
    def _():
        @pl.when(j == 0)
        def _():
            for s in range(TOK):
                xb_ref[:, s * LANES:(s + 1) * LANES] = (
                    x_ref[pl.ds(s, TM, stride=TOK), :].astype(BF16))
            acc_ref[...] = jnp.zeros_like(acc_ref)

        xb = xb_ref[...]
        hg = jnp.dot(xb, wg_ref[...].astype(BF16), preferred_element_type=F32)
        hu = jnp.dot(xb, wu_ref[...].astype(BF16), preferred_element_type=F32)
        acc_ref[...] += _bdot(jax.nn.silu(hg) * hu, wd_ref[...])

        @pl.when(j == pl.num_programs(1) - 1)
        def _():
            _to_token_tiles(y_ref, acc_ref, TM)


def gmm(xs, tile_expert, n_used, wg, wu, wd, l):
    rows = xs.shape[0] // TOK
    n_tiles = rows // TM
    nf = D_FF // TF

    def tile(i, nu):
        return jnp.minimum(i, nu[0] - 1)

    def ff(i, j, nu):
        return jnp.where(i < nu[0], j, nf - 1)

    grid_spec = pltpu.PrefetchScalarGridSpec(
        num_scalar_prefetch=2,
        grid=(n_tiles, nf),
        in_specs=[pl.BlockSpec((TM * TOK, LANES), lambda i, j, te, nu: (i, 0)),
                  pl.BlockSpec((None, None, D_MODEL, TF),
                               lambda i, j, te, nu: (l, te[tile(i, nu)], 0, ff(i, j, nu))),
                  pl.BlockSpec((None, None, D_MODEL, TF),
                               lambda i, j, te, nu: (l, te[tile(i, nu)], 0, ff(i, j, nu))),
                  pl.BlockSpec((None, None, TF, D_MODEL),
                               lambda i, j, te, nu: (l, te[tile(i, nu)], ff(i, j, nu), 0))],
        out_specs=pl.BlockSpec((TM * TOK, LANES), lambda i, j, te, nu: (i, 0)),
        scratch_shapes=[pltpu.VMEM((TM, D_MODEL), BF16), pltpu.VMEM((TM, D_MODEL), F32)],
    )
    return pl.pallas_call(
        _gmm_kernel,
        grid_spec=grid_spec,
        out_shape=jax.ShapeDtypeStruct(xs.shape, F32),
        compiler_params=_params("arbitrary", "arbitrary"),
        name="gmm",
    )(tile_expert, n_used, xs, wg, wu, wd)


def _combine_kernel(didx_ref, x_ref, prob_ref, g_ref, b_ref, y_ref, o_ref, ybuf, stage_ref, sem,
                    *, nb_seq):
    def issue(t, carry):
        for k in range(TOP_K):
            _token_copy(y_ref, ybuf, sem, didx_ref[0, TOP_K * t + k], k * TM + t, 1).start(
                priority=k % N_DMA_PRIORITIES)
        return carry

    lax.fori_loop(0, TM, issue, 0, unroll=ISSUE_UNROLL)
    for k in range(TOP_K):
        _token_copy(y_ref, ybuf, sem, 0, k * TM, TM).wait()
    p0 = prob_ref[:, 0:1]
    p1 = prob_ref[:, 1:2]
    f = jnp.concatenate(
        [p0 * ybuf[pl.ds(s, TM, stride=TOK), :] + p1 * ybuf[pl.ds(TM * TOK + s, TM, stride=TOK), :]
         for s in range(TOK)], axis=1)
    res = _layer_norm(ALPHA * x_ref[...] + f, g_ref[...], b_ref[...])
    _seq_major_from_rows(o_ref, stage_ref, res, nb_seq)


def combine(x, y, dest, prob, g, b, nb_seq):
    m = x.shape[0]
    nc = m // TM
    return pl.pallas_call(
        functools.partial(_combine_kernel, nb_seq=nb_seq),
        grid=(nc,),
        in_specs=[pl.BlockSpec((None, 1, TOP_K * TM), lambda i: (i, 0, 0),
                               memory_space=pltpu.SMEM),
                  pl.BlockSpec((TM, D_MODEL), lambda i: (i, 0)),
                  pl.BlockSpec((TM, LANES), lambda i: (i, 0)),
                  pl.BlockSpec((1, D_MODEL), lambda i: (0, 0)),
                  pl.BlockSpec((1, D_MODEL), lambda i: (0, 0)),
                  pl.BlockSpec(memory_space=pl.ANY)],
        out_specs=pl.BlockSpec((nb_seq, TM // nb_seq, D_MODEL), lambda i: (0, i, 0)),
        out_shape=jax.ShapeDtypeStruct((nb_seq, m // nb_seq, D_MODEL), F32),
        scratch_shapes=[pltpu.VMEM((TOP_K * TM * TOK, LANES), F32),
                        pltpu.VMEM((TOK, TM, LANES), F32), pltpu.SemaphoreType.DMA(())],
        compiler_params=_params("arbitrary"),
        name="combine",
    )(dest.reshape(nc, 1, TOP_K * TM), x, prob, g.reshape(1, -1), b.reshape(1, -1), y)


def moe_routed(xs_list, nb_seqs, p, l):
    j = l // 2
    routed = [router(x, p["w_router"][j], p["b_router"][j]) for x in xs_list]
    sel = jnp.concatenate([r[0] for r in routed], axis=0)
    n_pairs = TOP_K * sel.shape[0]
    n_tiles = n_pairs // TM + N_EXPERTS
    dest, tile_expert, n_used = _route(sel, TM, n_tiles)
    bounds = [0]
    for x in xs_list:
        bounds.append(bounds[-1] + TOP_K * x.shape[0])

    grouped = jnp.zeros((n_tiles * TM * TOK, LANES), F32)
    for x, lo, hi in zip(xs_list, bounds[:-1], bounds[1:]):
        grouped = dispatch(x, dest[lo:hi], grouped)
    y = gmm(grouped, tile_expert, n_used, p["w_e_gate"], p["w_e_up"], p["w_e_down"], j)
    return [combine(x, y, dest[lo:hi], prob, p["ln2_g"][l], p["ln2_b"][l], nb)
            for x, nb, (_, prob), lo, hi in zip(xs_list, nb_seqs, routed, bounds[:-1], bounds[1:])]


def _layer_weights(l, p):
    row = lambda a: a[l].reshape(1, -1)
    return {
        "b_in": row(p["b_in"]),
        "w_conv_a": p["w_conv_a"][l], "b_conv_a": row(p["b_conv_a"]),
        "ln_a_g": row(p["ln_a_g"]), "ln_a_b": row(p["ln_a_b"]),
        "w_a_out": p["w_a_out"][l].astype(BF16),
        "w_conv_b": p["w_conv_b"][l], "b_conv_b": row(p["b_conv_b"]),
        "w_rg": jnp.concatenate([p["w_rg_a"][l], p["w_rg_x"][l]], axis=-1).astype(BF16),
        "b_rg_a": row(p["b_rg_a"]), "b_rg_x": row(p["b_rg_x"]), "lam": row(p["lru_lambda"]),
        "w_b_out": p["w_b_out"][l].astype(BF16), "w_c_out": p["w_c_out"][l].astype(BF16),
        "w_mix_out": p["w_mix_out"][l].astype(BF16),
        "ln1_g": row(p["ln1_g"]), "ln1_b": row(p["ln1_b"]),
    }


def _mix_layer(l, w, w_in_bf, x, o, sa_tm, sb_tm, h0, tt, nb, x_bm):
    nbt, t = o.shape[:2]
    x1, nsa, nsb, hl = mix(l, o, x, sa_tm, sb_tm, h0, w, w_in_bf, tt, nb, x_bm)
    return x1.reshape(t * nbt, D_MODEL), nsa, nsb, hl


def _to_tm(a):
    return jnp.transpose(a, (1, 0, 2))


def kernel(x_prompt, x_sample, state_conv_a, state_conv_b, state_rglru, cache_mem_k, cache_mem_v, mem_prompt, w_in, b_in, w_conv_a, b_conv_a, ln_a_g, ln_a_b, w_a_out, w_conv_b, b_conv_b, w_rg_a, b_rg_a, w_rg_x, b_rg_x, lru_lambda, w_b_out, w_mem_kv, w_c_out, w_mix_out, ln1_g, ln1_b, w_ff_gate, w_ff_up, w_ff_down, w_router, b_router, w_e_gate, w_e_up, w_e_down, ln2_g, ln2_b):
    p = dict(w_in=w_in, b_in=b_in, w_conv_a=w_conv_a, b_conv_a=b_conv_a, ln_a_g=ln_a_g,
             ln_a_b=ln_a_b, w_a_out=w_a_out, w_conv_b=w_conv_b, b_conv_b=b_conv_b,
             w_rg_a=w_rg_a, b_rg_a=b_rg_a, w_rg_x=w_rg_x, b_rg_x=b_rg_x, lru_lambda=lru_lambda,
             w_b_out=w_b_out, w_c_out=w_c_out, w_mix_out=w_mix_out, ln1_g=ln1_g, ln1_b=ln1_b,
             w_ff_gate=w_ff_gate, w_ff_up=w_ff_up, w_ff_down=w_ff_down, w_router=w_router,
             b_router=b_router, w_e_gate=w_e_gate, w_e_up=w_e_up, w_e_down=w_e_down,
             ln2_g=ln2_g, ln2_b=ln2_b)
    bp, tp, _ = x_prompt.shape
    bs, ts, _ = x_sample.shape
    n_mem = mem_prompt.shape[1]

    xp_bm, xs_bm = x_prompt, x_sample
    xp_tm = xs_tm = None
    w_in_bf = w_in.astype(BF16)
    b_in3 = b_in.reshape(DEPTH, 1, D_IN)
    zero_a = jnp.zeros((K_A - 1, bp, D_A), F32)
    zero_b = jnp.zeros((K_B - 1, bp, D_B), F32)
    zero_h = jnp.zeros((bp, D_B), F32)
    zero_bias = jnp.zeros((DEPTH, 1, 2 * D_C), F32)
    cache_k = cache_mem_k.reshape(DEPTH, bs, n_mem * H_C, DH_C)
    cache_v = cache_mem_v.reshape(DEPTH, bs, n_mem * H_C, DH_C)
    outs = {k: [] for k in ("pa", "pb", "ph", "pk", "pv", "sa", "sb", "sh")}
    for l in range(DEPTH):
        w = _layer_weights(l, p)
        kv = matmul_bias(mem_prompt.reshape(bp * n_mem, D_MODEL), w_mem_kv, zero_bias, l,
                         Z_BLOCK, "kv_proj").reshape(bp, n_mem, 2 * D_C)
        mk = kv[..., :D_C]
        mv = kv[..., D_C:]
        kv4 = kv.reshape(1, bp, n_mem, 2 * D_C)
        op = attention(xp_bm, w_in_bf, b_in3, l, kv4, kv4, (0, 0, 1), tq=512)
        xp1, na, nb_, nh = _mix_layer(l, w, w_in_bf, xp_bm if xp_tm is None else xp_tm, op,
                                      zero_a, zero_b, zero_h, tt=64, nb=bp, x_bm=xp_tm is None)
        outs["pa"].append(_to_tm(na))
        outs["pb"].append(_to_tm(nb_))
        outs["ph"].append(nh)
        outs["pk"].append(mk.reshape(bp, n_mem, H_C, DH_C))
        outs["pv"].append(mv.reshape(bp, n_mem, H_C, DH_C))
        os_ = attention_rows(xs_bm, w_in_bf, b_in3, l, cache_k, cache_v, sb=8)
        xs1, na, nb_, nh = _mix_layer(l, w, w_in_bf, xs_bm if xs_tm is None else xs_tm, os_,
                                      _to_tm(state_conv_a[l]), _to_tm(state_conv_b[l]),
                                      state_rglru[l], tt=ts, nb=32, x_bm=xs_tm is None)
        outs["sa"].append(_to_tm(na))
        outs["sb"].append(_to_tm(nb_))
        outs["sh"].append(nh)
        if l % 2 == 0:
            j = l // 2
            (xp_rows, xp_bm), (xs_rows, xs_bm) = [
                ffn_dense(x, w_ff_gate, w_ff_up, w_ff_down, ln2_g[l], ln2_b[l], j, nb)
                for x, nb in ((xp1, bp), (xs1, bs))]
            xp_tm = xp_rows.reshape(tp, bp, D_MODEL)
            xs_tm = xs_rows.reshape(ts, bs, D_MODEL)
        else:
            xp_bm, xs_bm = moe_routed([xp1, xs1], [bp, bs], p, l)
            xp_tm = xs_tm = None

    st = lambda k: jnp.stack(outs[k])
    return (xp_bm, xs_bm, st("pa"), st("pb"), st("ph"), st("pk"), st("pv"),
            st("sa"), st("sb"), st("sh"))
```

```python
import functools

import jax
import jax.numpy as jnp
from jax import lax
from jax.experimental import pallas as pl
from jax.experimental.pallas import tpu as pltpu

D_MODEL = 1024
N_MEM = 256
D_A = 512
K_A = 31
D_B = 1024
K_B = 4
H_B = 8
BW_B = D_B // H_B
LRU_C = 8.0
D_C = 512
H_C = 4
DH_C = D_C // H_C
N_BRANCH = 3
D_IN = 2 * D_A + 2 * D_B + D_C + N_BRANCH * D_MODEL
D_FF = 2816
N_EXPERTS = 8
TOP_K = 2
DEPTH = 2
ALPHA = (2.0 * DEPTH) ** 0.25
LN_EPS = 1e-5

Z_BLOCK = 512
Q_BLOCK = (2 * D_A + 2 * D_B) // Z_BLOCK
ZM_COL = 2 * D_A + 2 * D_B + D_C

LANES = 128
VMEM_LIMIT = 56 * 1024 * 1024

TM = 1024
TF = 256
N_DMA_PRIORITIES = 2
ISSUE_UNROLL = 8
TOK = D_MODEL // LANES
assert TOK == 8

BF16 = jnp.bfloat16
F32 = jnp.float32


def _params(*sem):
    return pltpu.CompilerParams(dimension_semantics=sem, vmem_limit_bytes=VMEM_LIMIT)


def _layer_norm(x, g, b):
    mu = jnp.mean(x, axis=-1, keepdims=True)
    xc = x - mu
    var = jnp.mean(xc * xc, axis=-1, keepdims=True)
    return xc * lax.rsqrt(var + LN_EPS) * g + b


def _bdot(a, b):
    return jnp.dot(a.astype(BF16), b.astype(BF16), preferred_element_type=F32)


def _rows_from_seq_major(src_ref, stage_ref, nb):
    _, tt, c = src_ref.shape
    for s in range(nb):
        for k in range(c // LANES):
            stage_ref[k, pl.ds(s, tt, stride=nb), :] = src_ref[s, :, k * LANES:(k + 1) * LANES]
    return jnp.concatenate([stage_ref[k] for k in range(c // LANES)], axis=1)


def _seq_major_from_rows(dst_ref, stage_ref, rows, nb):
    r, c = rows.shape
    for k in range(c // LANES):
        stage_ref[k] = rows[:, k * LANES:(k + 1) * LANES]
    for s in range(nb):
        for k in range(c // LANES):
            dst_ref[s, :, k * LANES:(k + 1) * LANES] = stage_ref[k, pl.ds(s, r // nb, stride=nb), :]


def _matmul_bias_kernel(x_ref, w_ref, b_ref, o_ref, xb_ref):
    @pl.when(pl.program_id(1) == 0)
    def _():
        xb_ref[...] = x_ref[...].astype(BF16)

    o_ref[...] = jnp.dot(xb_ref[...], w_ref[...].astype(BF16),
                         preferred_element_type=F32) + b_ref[...]


def matmul_bias(x, w, b, l, tn, name):
    m, k = x.shape
    n = w.shape[2]
    return pl.pallas_call(
        _matmul_bias_kernel,
        grid=(m // TM, n // tn),
        in_specs=[pl.BlockSpec((TM, k), lambda i, j: (i, 0)),
                  pl.BlockSpec((None, k, tn), lambda i, j: (l, 0, j)),
                  pl.BlockSpec((None, 1, tn), lambda i, j: (l, 0, j))],
        out_specs=pl.BlockSpec((TM, tn), lambda i, j: (i, j)),
        out_shape=jax.ShapeDtypeStruct((m, n), F32),
        scratch_shapes=[pltpu.VMEM((TM, k), BF16)],
        compiler_params=_params("arbitrary", "arbitrary"),
        name=name,
    )(x, w, b)


def _attention_kernel(x_ref, wq_ref, bq_ref, k_ref, v_ref, o_ref):
    q = jnp.dot(x_ref[...].astype(BF16), wq_ref[...], preferred_element_type=F32) + bq_ref[...]
    for h in range(H_C):
        cols = slice(h * DH_C, (h + 1) * DH_C)
        kh = k_ref[:, cols].astype(BF16)
        vh = v_ref[:, cols].astype(BF16)
        sc = lax.dot_general(q[:, cols].astype(BF16), kh, (((1,), (1,)), ((), ())),
                             preferred_element_type=F32) * (DH_C ** -0.5)
        e = jnp.exp(sc - jnp.max(sc, axis=-1, keepdims=True))
        p = e * (1.0 / jnp.sum(e, axis=-1, keepdims=True))
        o_ref[:, cols] = jnp.dot(p.astype(BF16), vh, preferred_element_type=F32)


def attention(x, w_in_bf, b_in, l, k, v, kv_index, tq):
    s, t, _ = x.shape
    lk, kc, vc = kv_index
    return pl.pallas_call(
        _attention_kernel,
        grid=(s, t // tq),
        in_specs=[pl.BlockSpec((None, tq, D_MODEL), lambda i, j: (i, j, 0)),
                  pl.BlockSpec((None, D_MODEL, D_C), lambda i, j: (l, 0, Q_BLOCK)),
                  pl.BlockSpec((None, 1, D_C), lambda i, j: (l, 0, Q_BLOCK)),
                  pl.BlockSpec((None, None, N_MEM, D_C), lambda i, j: (lk, i, 0, kc)),
                  pl.BlockSpec((None, None, N_MEM, D_C), lambda i, j: (lk, i, 0, vc))],
        out_specs=pl.BlockSpec((None, tq, D_C), lambda i, j: (i, j, 0)),
        out_shape=jax.ShapeDtypeStruct((s, t, D_C), F32),
        compiler_params=_params("arbitrary", "arbitrary"),
        name="attention",
    )(x, w_in_bf, b_in, k, v)


def _attention_rows_kernel(x_ref, wq_ref, bq_ref, k_ref, v_ref, o_ref, *, sb, t):
    n_rows = H_C * t
    n_cols = N_MEM * H_C
    row_head = lax.broadcasted_iota(jnp.int32, (n_rows, n_cols), 0) // t
    col_head = lax.broadcasted_iota(jnp.int32, (n_rows, n_cols), 1) % H_C
    own = row_head == col_head
    qs = (jnp.dot(x_ref[...].reshape(sb * t, D_MODEL).astype(BF16), wq_ref[...],
                  preferred_element_type=F32) + bq_ref[...])
    for s in range(sb):
        q = qs[s * t:(s + 1) * t, :]
        qa = jnp.concatenate([q[:, h * DH_C:(h + 1) * DH_C] for h in range(H_C)], axis=0)
        sc = lax.dot_general(qa.astype(BF16), k_ref[s].astype(BF16), (((1,), (1,)), ((), ())),
                             preferred_element_type=F32) * (DH_C ** -0.5)
        sc = jnp.where(own, sc, -jnp.inf)
        e = jnp.exp(sc - jnp.max(sc, axis=-1, keepdims=True))
        p = e * (1.0 / jnp.sum(e, axis=-1, keepdims=True))
        oa = jnp.dot(p.astype(BF16), v_ref[s].astype(BF16), preferred_element_type=F32)
        o_ref[s] = jnp.concatenate([oa[h * t:(h + 1) * t, :] for h in range(H_C)], axis=1)


def attention_rows(x, w_in_bf, b_in, l, k, v, sb):
    s, t, _ = x.shape
    rows = N_MEM * H_C
    return pl.pallas_call(
        functools.partial(_attention_rows_kernel, sb=sb, t=t),
        grid=(s // sb,),
        in_specs=[pl.BlockSpec((sb, t, D_MODEL), lambda i: (i, 0, 0)),
                  pl.BlockSpec((None, D_MODEL, D_C), lambda i: (l, 0, Q_BLOCK)),
                  pl.BlockSpec((None, 1, D_C), lambda i: (l, 0, Q_BLOCK)),
                  pl.BlockSpec((None, sb, rows, DH_C), lambda i: (l, i, 0, 0)),
                  pl.BlockSpec((None, sb, rows, DH_C), lambda i: (l, i, 0, 0))],
        out_specs=pl.BlockSpec((sb, t, D_C), lambda i: (i, 0, 0)),
        out_shape=jax.ShapeDtypeStruct((s, t, D_C), F32),
        compiler_params=_params("arbitrary"),
        name="attention_rows",
    )(x, w_in_bf, b_in, k, v)


def _mix_kernel(o_ref, x_ref, sa_ref, sb_ref, h0_ref, bin_ref,
                wca_ref, bca_ref, lag_ref, lab_ref, wao_ref,
                wcb_ref, bcb_ref, wrg_ref, bra_ref, brx_ref, lam_ref, wbo_ref,
                wco_ref, wmo_ref, l1g_ref, l1b_ref, win_hbm,
                x1_ref, nsa_ref, nsb_ref, hl_ref,
                ua_buf, xb_buf, a_buf, u_buf, hs_buf, h_carry, o_buf, x_buf, win_buf, win_sem,
                *, l, tt, nb, nt, x_bm):
    r = tt * nb
    ha = (K_A - 1) * nb
    hb = (K_B - 1) * nb
    i = pl.program_id(1)

    @pl.when((pl.program_id(0) == 0) & (i == 0))
    def _():
        load = pltpu.make_async_copy(win_hbm.at[l], win_buf, win_sem)
        load.start()
        load.wait()

    @pl.when(i == 0)
    def _():
        ua_buf[0:ha, :] = sa_ref[...].reshape(ha, D_A)
        xb_buf[0:hb, :] = sb_ref[...].reshape(hb, D_B)
        h_carry[...] = h0_ref[...]

    if x_bm:
        x = _rows_from_seq_major(x_ref, x_buf, nb)
    else:
        x = x_ref[...].reshape(r, D_MODEL)
    xb = x.astype(BF16)

    def zcols(lo, hi):
        return (jnp.dot(xb, win_buf[:, lo:hi], preferred_element_type=F32) + bin_ref[:, lo:hi])

    za = zcols(0, 2 * D_A)
    ua_buf[ha:ha + r, :] = za[:, :D_A] * jax.nn.sigmoid(za[:, D_A:])
    later_cols = [(2 * D_A + n * D_B, 2 * D_A + (n + 1) * D_B) for n in range(2)]
    later_cols += [(ZM_COL + n * D_MODEL, ZM_COL + (n + 1) * D_MODEL) for n in range(N_BRANCH)]
    taps_per_group = -(-K_A // len(later_cols))
    z_later = []
    c_a = jnp.broadcast_to(bca_ref[...], (r, D_A))
    for n, (lo, hi) in enumerate(later_cols):
        for k in range(n * taps_per_group, min((n + 1) * taps_per_group, K_A)):
            c_a = c_a + wca_ref[k:k + 1, :] * ua_buf[k * nb:k * nb + r, :]
        z_later.append(zcols(lo, hi))
    z_b, z_g, z_m0, z_m1, z_m2 = z_later
    y_a = _bdot(jax.nn.silu(_layer_norm(c_a, lag_ref[...], lab_ref[...])), wao_ref[...])
    nsa_ref[...] = ua_buf[r:r + ha, :].reshape(K_A - 1, nb, D_A)

    xb_buf[hb:hb + r, :] = z_b
    c_b = jnp.broadcast_to(bcb_ref[...], (r, D_B))
    for k in range(K_B):
        c_b = c_b + wcb_ref[k:k + 1, :] * xb_buf[k * nb:k * nb + r, :]
    nsb_ref[...] = xb_buf[r:r + hb, :].reshape(K_B - 1, nb, D_B)
    c_bf = c_b.astype(BF16)
    ra, rx = [], []
    for h in range(H_B):
        gh = jnp.dot(c_bf[:, h * BW_B:(h + 1) * BW_B], wrg_ref[h], preferred_element_type=F32)
        ra.append(gh[:, :BW_B])
        rx.append(gh[:, BW_B:])
    gate_r = jax.nn.sigmoid(jnp.concatenate(ra, axis=1) + bra_ref[...])
    gate_i = jax.nn.sigmoid(jnp.concatenate(rx, axis=1) + brx_ref[...])
    log_a = (-LRU_C) * gate_r * jax.nn.softplus(-lam_ref[...])
    a_buf[...] = jnp.exp(log_a)
    th = jnp.tanh(log_a)
    u_buf[...] = jnp.sqrt(-2.0 * th / (1.0 - th)) * (gate_i * c_b)
    h = h_carry[...]
    for t in range(tt):
        rows = slice(t * nb, (t + 1) * nb)
        h = a_buf[rows, :] * h + u_buf[rows, :]
        hs_buf[rows, :] = h
    h_carry[...] = h
    hl_ref[...] = h
    y_b = _bdot(hs_buf[...] * jax.nn.gelu(z_g), wbo_ref[...])

    if nt > 1:
        ua_buf[0:ha, :] = ua_buf[r:r + ha, :]
        xb_buf[0:hb, :] = xb_buf[r:r + hb, :]

    y_c = _bdot(_rows_from_seq_major(o_ref, o_buf, nb), wco_ref[...])

    merged = jax.nn.sigmoid(z_m0) * y_a
    merged = merged + jax.nn.sigmoid(z_m1) * y_b
    merged = merged + jax.nn.sigmoid(z_m2) * y_c
    mix = _bdot(merged, wmo_ref[...])
    x1_ref[...] = _layer_norm(ALPHA * x + mix, l1g_ref[...], l1b_ref[...]).reshape(tt, nb, D_MODEL)


def mix(l, o, x, sa_tm, sb_tm, h0, w, w_in_bf, tt, nb, x_bm):
    nbt, t = o.shape[:2]
    nt = t // tt
    assert nt == 1 or tt >= K_A - 1
    r = tt * nb
    grid = (nbt // nb, nt)

    def bm_spec(width):
        return pl.BlockSpec((nb, tt, width), lambda g, i: (g, i, 0))

    def zspec(width, blk):
        return pl.BlockSpec((tt, nb, width), lambda g, i: (i, g, blk))

    def full(a):
        nd = a.ndim
        return pl.BlockSpec(a.shape, lambda g, i: (0,) * nd)

    weights = [w["b_in"], w["w_conv_a"], w["b_conv_a"], w["ln_a_g"], w["ln_a_b"], w["w_a_out"],
               w["w_conv_b"], w["b_conv_b"], w["w_rg"], w["b_rg_a"], w["b_rg_x"], w["lam"],
               w["w_b_out"], w["w_c_out"], w["w_mix_out"], w["ln1_g"], w["ln1_b"]]
    in_specs = ([bm_spec(D_C), bm_spec(D_MODEL) if x_bm else zspec(D_MODEL, 0),
                 pl.BlockSpec((K_A - 1, nb, D_A), lambda g, i: (0, g, 0)),
                 pl.BlockSpec((K_B - 1, nb, D_B), lambda g, i: (0, g, 0)),
                 pl.BlockSpec((nb, D_B), lambda g, i: (g, 0))]
                + [full(a) for a in weights]
                + [pl.BlockSpec(memory_space=pl.ANY)])
    out_specs = [zspec(D_MODEL, 0),
                 pl.BlockSpec((K_A - 1, nb, D_A), lambda g, i: (0, g, 0)),
                 pl.BlockSpec((K_B - 1, nb, D_B), lambda g, i: (0, g, 0)),
                 pl.BlockSpec((nb, D_B), lambda g, i: (g, 0))]
    out_shape = [jax.ShapeDtypeStruct((t, nbt, D_MODEL), F32),
                 jax.ShapeDtypeStruct((K_A - 1, nbt, D_A), F32),
                 jax.ShapeDtypeStruct((K_B - 1, nbt, D_B), F32),
                 jax.ShapeDtypeStruct((nbt, D_B), F32)]
    scratch = [pltpu.VMEM(((K_A - 1) * nb + r, D_A), F32),
               pltpu.VMEM(((K_B - 1) * nb + r, D_B), F32),
               pltpu.VMEM((r, D_B), F32), pltpu.VMEM((r, D_B), F32), pltpu.VMEM((r, D_B), F32),
               pltpu.VMEM((nb, D_B), F32), pltpu.VMEM((D_C // LANES, r, LANES), F32),
               pltpu.VMEM((D_MODEL // LANES, r, LANES), F32),
               pltpu.VMEM((D_MODEL, D_IN), BF16), pltpu.SemaphoreType.DMA(())]
    return pl.pallas_call(
        functools.partial(_mix_kernel, l=l, tt=tt, nb=nb, nt=nt, x_bm=x_bm),
        grid=grid, in_specs=in_specs, out_specs=out_specs, out_shape=out_shape,
        scratch_shapes=scratch,
        compiler_params=_params("arbitrary", "arbitrary"),
        name="mix",
    )(o, x, sa_tm, sb_tm, h0, *weights, w_in_bf)


def _ffn_kernel(x_ref, wg_ref, wu_ref, wd_ref, g_ref, b_ref, o_ref, obm_ref, xb_ref, acc_ref,
                stage_ref, *, nb_seq):
    j = pl.program_id(1)

    @pl.when(j == 0)
    def _():
        xb_ref[...] = x_ref[...].astype(BF16)
        acc_ref[...] = jnp.zeros_like(acc_ref)

    xb = xb_ref[...]
    hg = jnp.dot(xb, wg_ref[...].astype(BF16), preferred_element_type=F32)
    hu = jnp.dot(xb, wu_ref[...].astype(BF16), preferred_element_type=F32)
    acc_ref[...] += _bdot(jax.nn.silu(hg) * hu, wd_ref[...])

    @pl.when(j == pl.num_programs(1) - 1)
    def _():
        res = _layer_norm(ALPHA * x_ref[...] + acc_ref[...], g_ref[...], b_ref[...])
        o_ref[...] = res
        _seq_major_from_rows(obm_ref, stage_ref, res, nb_seq)


def ffn_dense(x, wg, wu, wd, g, b, l, nb_seq):
    m = x.shape[0]
    return pl.pallas_call(
        functools.partial(_ffn_kernel, nb_seq=nb_seq),
        grid=(m // TM, D_FF // TF),
        in_specs=[pl.BlockSpec((TM, D_MODEL), lambda i, j: (i, 0)),
                  pl.BlockSpec((None, D_MODEL, TF), lambda i, j: (l, 0, j)),
                  pl.BlockSpec((None, D_MODEL, TF), lambda i, j: (l, 0, j)),
                  pl.BlockSpec((None, TF, D_MODEL), lambda i, j: (l, j, 0)),
                  pl.BlockSpec((1, D_MODEL), lambda i, j: (0, 0)),
                  pl.BlockSpec((1, D_MODEL), lambda i, j: (0, 0))],
        out_specs=[pl.BlockSpec((TM, D_MODEL), lambda i, j: (i, 0)),
                   pl.BlockSpec((nb_seq, TM // nb_seq, D_MODEL), lambda i, j: (0, i, 0))],
        out_shape=[jax.ShapeDtypeStruct((m, D_MODEL), F32),
                   jax.ShapeDtypeStruct((nb_seq, m // nb_seq, D_MODEL), F32)],
        scratch_shapes=[pltpu.VMEM((TM, D_MODEL), BF16), pltpu.VMEM((TM, D_MODEL), F32),
                        pltpu.VMEM((TOK, TM, LANES), F32)],
        compiler_params=_params("arbitrary", "arbitrary"),
        name="ffn_dense",
    )(x, wg, wu, wd, g.reshape(1, -1), b.reshape(1, -1))


def _router_kernel(x_ref, w_ref, b_ref, sel_ref, prob_ref):
    x = x_ref[...]
    w = w_ref[...]
    xh = x.astype(BF16)
    wh = w.astype(BF16)
    xl = (x - xh.astype(F32)).astype(BF16)
    wl = (w - wh.astype(F32)).astype(BF16)
    dot = functools.partial(jnp.dot, preferred_element_type=F32)
    logits = dot(xh, wh) + (dot(xh, wl) + dot(xl, wh) + dot(xl, wl)) + b_ref[...]
    lane = lax.broadcasted_iota(jnp.int32, logits.shape, 1)
    neg = jnp.float32(-jnp.inf)
    l1 = jnp.where(lane < N_EXPERTS, logits, neg)
    m1 = jnp.max(l1, axis=1, keepdims=True)
    i1 = jnp.min(jnp.where(l1 == m1, lane, LANES), axis=1, keepdims=True)
    l2 = jnp.where(lane == i1, neg, l1)
    m2 = jnp.max(l2, axis=1, keepdims=True)
    i2 = jnp.min(jnp.where(l2 == m2, lane, LANES), axis=1, keepdims=True)
    e2 = jnp.exp(m2 - m1)
    den = 1.0 + e2
    sel_ref[...] = jnp.where(lane == 0, i1, jnp.where(lane == 1, i2, 0))
    prob_ref[...] = jnp.where(lane == 0, 1.0 / den, jnp.where(lane == 1, e2 / den, 0.0))


def router(x, w_router, b_router):
    m = x.shape[0]
    wp = jnp.zeros((D_MODEL, LANES), F32).at[:, :N_EXPERTS].set(w_router)
    bp = jnp.zeros((1, LANES), F32).at[0, :N_EXPERTS].set(b_router)
    return pl.pallas_call(
        _router_kernel,
        grid=(m // TM,),
        in_specs=[pl.BlockSpec((TM, D_MODEL), lambda i: (i, 0)),
                  pl.BlockSpec((D_MODEL, LANES), lambda i: (0, 0)),
                  pl.BlockSpec((1, LANES), lambda i: (0, 0))],
        out_specs=[pl.BlockSpec((TM, LANES), lambda i: (i, 0)),
                   pl.BlockSpec((TM, LANES), lambda i: (i, 0))],
        out_shape=[jax.ShapeDtypeStruct((m, LANES), jnp.int32),
                   jax.ShapeDtypeStruct((m, LANES), F32)],
        compiler_params=_params("arbitrary"),
        name="router",
    )(x, wp, bp)


def _route(sel, tm, n_tiles):
    e = sel[:, :TOP_K].reshape(-1)
    onehot = (e[:, None] == jnp.arange(N_EXPERTS, dtype=jnp.int32)[None, :]).astype(jnp.int32)
    csum = jnp.cumsum(onehot, axis=0)
    rank = jnp.sum(onehot * csum, axis=1) - 1
    counts = csum[-1]
    padded = ((counts + tm - 1) // tm) * tm
    ends = jnp.cumsum(padded)
    starts = ends - padded
    dest = jnp.sum(onehot * starts[None, :], axis=1) + rank
    first_row = jnp.arange(n_tiles, dtype=jnp.int32) * tm
    tile_expert = jnp.sum((first_row[:, None] >= ends[None, :]).astype(jnp.int32), axis=1)
    tile_expert = jnp.minimum(tile_expert, N_EXPERTS - 1)
    n_used = (ends[-1] // tm).reshape(1)
    return dest.astype(jnp.int32), tile_expert.astype(jnp.int32), n_used.astype(jnp.int32)


def _token_copy(src_ref, dst_ref, sem, s_tok, d_tok, n_tok):
    s0 = pl.multiple_of(s_tok * TOK, TOK)
    d0 = pl.multiple_of(d_tok * TOK, TOK)
    return pltpu.make_async_copy(src_ref.at[pl.ds(s0, n_tok * TOK), :],
                                 dst_ref.at[pl.ds(d0, n_tok * TOK), :], sem)


def _to_token_tiles(dst_ref, src, rows):
    for s in range(TOK):
        dst_ref[pl.ds(s, rows, stride=TOK), :] = src[:, s * LANES:(s + 1) * LANES]


def _dispatch_kernel(didx_ref, x_ref, init_ref, out_ref, tok_buf, sems, *, nc):
    del init_ref
    c = pl.program_id(0)
    slot = c % 2 if nc > 1 else 0
    buf = tok_buf.at[slot]
    _to_token_tiles(buf, x_ref[...], TM)

    def issue(t, carry):
        for k in range(TOP_K):
            _token_copy(buf, out_ref, sems.at[slot], t, didx_ref[0, TOP_K * t + k], 1).start(
                priority=k % N_DMA_PRIORITIES)
        return carry

    lax.fori_loop(0, TM, issue, 0)

    def drain(s):
        for _ in range(TOP_K):
            _token_copy(tok_buf.at[s], out_ref, sems.at[s], 0, 0, TM).wait()

    if nc == 1:
        drain(slot)
    else:
        @pl.when(c > 0)
        def _():
            drain(1 - slot)

        @pl.when(c == nc - 1)
        def _():
            drain(slot)


def dispatch(x, dest, grouped):
    m = x.shape[0]
    nc = m // TM
    return pl.pallas_call(
        functools.partial(_dispatch_kernel, nc=nc),
        grid=(nc,),
        in_specs=[pl.BlockSpec((None, 1, TOP_K * TM), lambda c: (c, 0, 0),
                               memory_space=pltpu.SMEM),
                  pl.BlockSpec((TM, D_MODEL), lambda c: (c, 0)),
                  pl.BlockSpec(memory_space=pl.ANY)],
        out_specs=pl.BlockSpec(memory_space=pl.ANY),
        out_shape=jax.ShapeDtypeStruct(grouped.shape, grouped.dtype),
        scratch_shapes=[pltpu.VMEM((2, TM * TOK, LANES), F32), pltpu.SemaphoreType.DMA((2,))],
        input_output_aliases={2: 0},
        compiler_params=_params("arbitrary"),
        name="dispatch",
    )(dest.reshape(nc, 1, TOP_K * TM), x, grouped)


def _gmm_kernel(te_ref, nu_ref, x_ref, wg_ref, wu_ref, wd_ref, y_ref, xb_ref, acc_ref):
    del te_ref
    i = pl.program_id(0)
    j = pl.program_id(1)

    @pl.when((i >= nu_ref[0]) & (j == 0))
    def _():
        y_ref[...] = jnp.zeros_like(y_ref)

    @pl.when(i < nu_ref[0])
    def _():
        @pl.when(j == 0)
        def _():
            for s in range(TOK):
                xb_ref[:, s * LANES:(s + 1) * LANES] = (
                    x_ref[pl.ds(s, TM, stride=TOK), :].astype(BF16))
            acc_ref[...] = jnp.zeros_like(acc_ref)

        xb = xb_ref[...]
        hg = jnp.dot(xb, wg_ref[...].astype(BF16), preferred_element_type=F32)
        hu = jnp.dot(xb, wu_ref[...].astype(BF16), preferred_element_type=F32)
        acc_ref[...] += _bdot(jax.nn.silu(hg) * hu, wd_ref[...])

        @pl.when(j == pl.num_programs(1) - 1)
        def _():
            _to_token_tiles(y_ref, acc_ref, TM)


def gmm(xs, tile_expert, n_used, wg, wu, wd, l):
    rows = xs.shape[0] // TOK
    n_tiles = rows // TM
    nf = D_FF // TF

    def tile(i, nu):
        return jnp.minimum(i, nu[0] - 1)

    def ff(i, j, nu):
        return jnp.where(i < nu[0], j, nf - 1)

    grid_spec = pltpu.PrefetchScalarGridSpec(
        num_scalar_prefetch=2,
        grid=(n_tiles, nf),
        in_specs=[pl.BlockSpec((TM * TOK, LANES), lambda i, j, te, nu: (i, 0)),
                  pl.BlockSpec((None, None, D_MODEL, TF),
                               lambda i, j, te, nu: (l, te[tile(i, nu)], 0, ff(i, j, nu))),
                  pl.BlockSpec((None, None, D_MODEL, TF),
                               lambda i, j, te, nu: (l, te[tile(i, nu)], 0, ff(i, j, nu))),
                  pl.BlockSpec((None, None, TF, D_MODEL),
                               lambda i, j, te, nu: (l, te[tile(i, nu)], ff(i, j, nu), 0))],
        out_specs=pl.BlockSpec((TM * TOK, LANES), lambda i, j, te, nu: (i, 0)),
        scratch_shapes=[pltpu.VMEM((TM, D_MODEL), BF16), pltpu.VMEM((TM, D_MODEL), F32)],
    )
    return pl.pallas_call(
        _gmm_kernel,
        grid_spec=grid_spec,
        out_shape=jax.ShapeDtypeStruct(xs.shape, F32),
        compiler_params=_params("arbitrary", "arbitrary"),
        name="gmm",
    )(tile_expert, n_used, xs, wg, wu, wd)


def _combine_kernel(didx_ref, x_ref, prob_ref, g_ref, b_ref, y_ref, o_ref, ybuf, stage_ref, sem,
                    *, nb_seq):
    def issue(t, carry):
        for k in range(TOP_K):
            _token_copy(y_ref, ybuf, sem, didx_ref[0, TOP_K * t + k], k * TM + t, 1).start(
                priority=k % N_DMA_PRIORITIES)
        return carry

    lax.fori_loop(0, TM, issue, 0, unroll=ISSUE_UNROLL)
    for k in range(TOP_K):
        _token_copy(y_ref, ybuf, sem, 0, k * TM, TM).wait()
    p0 = prob_ref[:, 0:1]
    p1 = prob_ref[:, 1:2]
    f = jnp.concatenate(
        [p0 * ybuf[pl.ds(s, TM, stride=TOK), :] + p1 * ybuf[pl.ds(TM * TOK + s, TM, stride=TOK), :]
         for s in range(TOK)], axis=1)
    res = _layer_norm(ALPHA * x_ref[...] + f, g_ref[...], b_ref[...])
    _seq_major_from_rows(o_ref, stage_ref, res, nb_seq)


def combine(x, y, dest, prob, g, b, nb_seq):
    m = x.shape[0]
    nc = m // TM
    return pl.pallas_call(
        functools.partial(_combine_kernel, nb_seq=nb_seq),
        grid=(nc,),
        in_specs=[pl.BlockSpec((None, 1, TOP_K * TM), lambda i: (i, 0, 0),
                               memory_space=pltpu.SMEM),
                  pl.BlockSpec((TM, D_MODEL), lambda i: (i, 0)),
                  pl.BlockSpec((TM, LANES), lambda i: (i, 0)),
                  pl.BlockSpec((1, D_MODEL), lambda i: (0, 0)),
                  pl.BlockSpec((1, D_MODEL), lambda i: (0, 0)),
                  pl.BlockSpec(memory_space=pl.ANY)],
        out_specs=pl.BlockSpec((nb_seq, TM // nb_seq, D_MODEL), lambda i: (0, i, 0)),
        out_shape=jax.ShapeDtypeStruct((nb_seq, m // nb_seq, D_MODEL), F32),
        scratch_shapes=[pltpu.VMEM((TOP_K * TM * TOK, LANES), F32),
                        pltpu.VMEM((TOK, TM, LANES), F32), pltpu.SemaphoreType.DMA(())],
        compiler_params=_params("arbitrary"),
        name="combine",
    )(dest.reshape(nc, 1, TOP_K * TM), x, prob, g.reshape(1, -1), b.reshape(1, -1), y)


def moe_routed(xs_list, nb_seqs, p, l):
    j = l // 2
    routed = [router(x, p["w_router"][j], p["b_router"][j]) for x in xs_list]
    sel = jnp.concatenate([r[0] for r in routed], axis=0)
    n_pairs = TOP_K * sel.shape[0]
    n_tiles = n_pairs // TM + N_EXPERTS
    dest, tile_expert, n_used = _route(sel, TM, n_tiles)
    bounds = [0]
    for x in xs_list:
        bounds.append(bounds[-1] + TOP_K * x.shape[0])

    grouped = jnp.zeros((n_tiles * TM * TOK, LANES), F32)
    for x, lo, hi in zip(xs_list, bounds[:-1], bounds[1:]):
        grouped = dispatch(x, dest[lo:hi], grouped)
    y = gmm(grouped, tile_expert, n_used, p["w_e_gate"], p["w_e_up"], p["w_e_down"], j)
    return [combine(x, y, dest[lo:hi], prob, p["ln2_g"][l], p["ln2_b"][l], nb)
            for x, nb, (_, prob), lo, hi in zip(xs_list, nb_seqs, routed, bounds[:-1], bounds[1:])]


def _layer_weights(l, p):
    row = lambda a: a[l].reshape(1, -1)
    return {
        "b_in": row(p["b_in"]),
        "w_conv_a": p["w_conv_a"][l], "b_conv_a": row(p["b_conv_a"]),
        "ln_a_g": row(p["ln_a_g"]), "ln_a_b": row(p["ln_a_b"]),
        "w_a_out": p["w_a_out"][l].astype(BF16),
        "w_conv_b": p["w_conv_b"][l], "b_conv_b": row(p["b_conv_b"]),
        "w_rg": jnp.concatenate([p["w_rg_a"][l], p["w_rg_x"][l]], axis=-1).astype(BF16),
        "b_rg_a": row(p["b_rg_a"]), "b_rg_x": row(p["b_rg_x"]), "lam": row(p["lru_lambda"]),
        "w_b_out": p["w_b_out"][l].astype(BF16), "w_c_out": p["w_c_out"][l].astype(BF16),
        "w_mix_out": p["w_mix_out"][l].astype(BF16),
        "ln1_g": row(p["ln1_g"]), "ln1_b": row(p["ln1_b"]),
    }


def _mix_layer(l, w, w_in_bf, x, o, sa_tm, sb_tm, h0, tt, nb, x_bm):
    nbt, t = o.shape[:2]
    x1, nsa, nsb, hl = mix(l, o, x, sa_tm, sb_tm, h0, w, w_in_bf, tt, nb, x_bm)
    return x1.reshape(t * nbt, D_MODEL), nsa, nsb, hl


def _to_tm(a):
    return jnp.transpose(a, (1, 0, 2))


def kernel(x_prompt, x_sample, state_conv_a, state_conv_b, state_rglru, cache_mem_k, cache_mem_v, mem_prompt, w_in, b_in, w_conv_a, b_conv_a, ln_a_g, ln_a_b, w_a_out, w_conv_b, b_conv_b, w_rg_a, b_rg_a, w_rg_x, b_rg_x, lru_lambda, w_b_out, w_mem_kv, w_c_out, w_mix_out, ln1_g, ln1_b, w_ff_gate, w_ff_up, w_ff_down, w_router, b_router, w_e_gate, w_e_up, w_e_down, ln2_g, ln2_b):
    p = dict(w_in=w_in, b_in=b_in, w_conv_a=w_conv_a, b_conv_a=b_conv_a, ln_a_g=ln_a_g,
             ln_a_b=ln_a_b, w_a_out=w_a_out, w_conv_b=w_conv_b, b_conv_b=b_conv_b,
             w_rg_a=w_rg_a, b_rg_a=b_rg_a, w_rg_x=w_rg_x, b_rg_x=b_rg_x, lru_lambda=lru_lambda,
             w_b_out=w_b_out, w_c_out=w_c_out, w_mix_out=w_mix_out, ln1_g=ln1_g, ln1_b=ln1_b,
             w_ff_gate=w_ff_gate, w_ff_up=w_ff_up, w_ff_down=w_ff_down, w_router=w_router,
             b_router=b_router, w_e_gate=w_e_gate, w_e_up=w_e_up, w_e_down=w_e_down,
             ln2_g=ln2_g, ln2_b=ln2_b)
    bp, tp, _ = x_prompt.shape
    bs, ts, _ = x_sample.shape
    n_mem = mem_prompt.shape[1]

    xp_bm, xs_bm = x_prompt, x_sample
    xp_tm = xs_tm = None
    w_in_bf = w_in.astype(BF16)
    b_in3 = b_in.reshape(DEPTH, 1, D_IN)
    zero_a = jnp.zeros((K_A - 1, bp, D_A), F32)
    zero_b = jnp.zeros((K_B - 1, bp, D_B), F32)
    zero_h = jnp.zeros((bp, D_B), F32)
    zero_bias = jnp.zeros((DEPTH, 1, 2 * D_C), F32)
    cache_k = cache_mem_k.reshape(DEPTH, bs, n_mem * H_C, DH_C)
    cache_v = cache_mem_v.reshape(DEPTH, bs, n_mem * H_C, DH_C)
    outs = {k: [] for k in ("pa", "pb", "ph", "pk", "pv", "sa", "sb", "sh")}
    for l in range(DEPTH):
        w = _layer_weights(l, p)
        kv = matmul_bias(mem_prompt.reshape(bp * n_mem, D_MODEL), w_mem_kv, zero_bias, l,
                         Z_BLOCK, "kv_proj").reshape(bp, n_mem, 2 * D_C)
        mk = kv[..., :D_C]
        mv = kv[..., D_C:]
        kv4 = kv.reshape(1, bp, n_mem, 2 * D_C)
        op = attention(xp_bm, w_in_bf, b_in3, l, kv4, kv4, (0, 0, 1), tq=512)
        xp1, na, nb_, nh = _mix_layer(l, w, w_in_bf, xp_bm if xp_tm is None else xp_tm, op,
                                      zero_a, zero_b, zero_h, tt=64, nb=bp, x_bm=xp_tm is None)
        outs["pa"].append(_to_tm(na))
        outs["pb"].append(_to_tm(nb_))
        outs["ph"].append(nh)
        outs["pk"].append(mk.reshape(bp, n_mem, H_C, DH_C))
        outs["pv"].append(mv.reshape(bp, n_mem, H_C, DH_C))
        os_ = attention_rows(xs_bm, w_in_bf, b_in3, l, cache_k, cache_v, sb=8)
        xs1, na, nb_, nh = _mix_layer(l, w, w_in_bf, xs_bm if xs_tm is None else xs_tm, os_,
                                      _to_tm(state_conv_a[l]), _to_tm(state_conv_b[l]),
                                      state_rglru[l], tt=ts, nb=32, x_bm=xs_tm is None)
        outs["sa"].append(_to_tm(na))
        outs["sb"].append(_to_tm(nb_))
        outs["sh"].append(nh)
        if l % 2 == 0:
            j = l // 2
            (xp_rows, xp_bm), (xs_rows, xs_bm) = [
                ffn_dense(x, w_ff_gate, w_ff_up, w_ff_down, ln2_g[l], ln2_b[l], j, nb)
                for x, nb in ((xp1, bp), (xs1, bs))]
            xp_tm = xp_rows.reshape(tp, bp, D_MODEL)
            xs_tm = xs_rows.reshape(ts, bs, D_MODEL)
        else:
            xp_bm, xs_bm = moe_routed([xp1, xs1], [bp, bs], p, l)
            xp_tm = xs_tm = None

    st = lambda k: jnp.stack(outs[k])
    return (xp_bm, xs_bm, st("pa"), st("pb"), st("ph"), st("pk"), st("pv"),
            st("sa"), st("sb"), st("sh"))
```

```python
import functools

import jax
import jax.numpy as jnp
from jax import lax
from jax.experimental import pallas as pl
from jax.experimental.pallas import tpu as pltpu

D_MODEL = 1024
N_MEM = 256
D_A = 512
K_A = 31
D_B = 1024
K_B = 4
H_B = 8
BW_B = D_B // H_B
LRU_C = 8.0
D_C = 512
H_C = 4
DH_C = D_C // H_C
N_BRANCH = 3
D_IN = 2 * D_A + 2 * D_B + D_C + N_BRANCH * D_MODEL
D_FF = 2816
N_EXPERTS = 8
TOP_K = 2
DEPTH = 2
ALPHA = (2.0 * DEPTH) ** 0.25
LN_EPS = 1e-5

Z_BLOCK = 512
Q_BLOCK = (2 * D_A + 2 * D_B) // Z_BLOCK
ZM_COL = 2 * D_A + 2 * D_B + D_C

LANES = 128
VMEM_LIMIT = 56 * 1024 * 1024

TM = 1024
TF = 256
N_DMA_PRIORITIES = 2
ISSUE_UNROLL = 8
TOK = D_MODEL // LANES
assert TOK == 8

BF16 = jnp.bfloat16
F32 = jnp.float32


def _params(*sem):
    return pltpu.CompilerParams(dimension_semantics=sem, vmem_limit_bytes=VMEM_LIMIT)


def _layer_norm(x, g, b):
    mu = jnp.mean(x, axis=-1, keepdims=True)
    xc = x - mu
    var = jnp.mean(xc * xc, axis=-1, keepdims=True)
    return xc * lax.rsqrt(var + LN_EPS) * g + b


def _bdot(a, b):
    return jnp.dot(a.astype(BF16), b.astype(BF16), preferred_element_type=F32)


def _rows_from_seq_major(src_ref, stage_ref, nb):
    _, tt, c = src_ref.shape
    for s in range(nb):
        for k in range(c // LANES):
            stage_ref[k, pl.ds(s, tt, stride=nb), :] = src_ref[s, :, k * LANES:(k + 1) * LANES]
    return jnp.concatenate([stage_ref[k] for k in range(c // LANES)], axis=1)


def _seq_major_from_rows(dst_ref, stage_ref, rows, nb):
    r, c = rows.shape
    for k in range(c // LANES):
        stage_ref[k] = rows[:, k * LANES:(k + 1) * LANES]
    for s in range(nb):
        for k in range(c // LANES):
            dst_ref[s, :, k * LANES:(k + 1) * LANES] = stage_ref[k, pl.ds(s, r // nb, stride=nb), :]


def _matmul_bias_kernel(x_ref, w_ref, b_ref, o_ref, xb_ref):
    @pl.when(pl.program_id(1) == 0)
    def _():
        xb_ref[...] = x_ref[...].astype(BF16)

    o_ref[...] = jnp.dot(xb_ref[...], w_ref[...].astype(BF16),
                         preferred_element_type=F32) + b_ref[...]


def matmul_bias(x, w, b, l, tn, name):
    m, k = x.shape
    n = w.shape[2]
    return pl.pallas_call(
        _matmul_bias_kernel,
        grid=(m // TM, n // tn),
        in_specs=[pl.BlockSpec((TM, k), lambda i, j: (i, 0)),
                  pl.BlockSpec((None, k, tn), lambda i, j: (l, 0, j)),
                  pl.BlockSpec((None, 1, tn), lambda i, j: (l, 0, j))],
        out_specs=pl.BlockSpec((TM, tn), lambda i, j: (i, j)),
        out_shape=jax.ShapeDtypeStruct((m, n), F32),
        scratch_shapes=[pltpu.VMEM((TM, k), BF16)],
        compiler_params=_params("arbitrary", "arbitrary"),
        name=name,
    )(x, w, b)


def _attention_kernel(x_ref, wq_ref, bq_ref, k_ref, v_ref, o_ref):
    q = jnp.dot(x_ref[...].astype(BF16), wq_ref[...], preferred_element_type=F32) + bq_ref[...]
    for h in range(H_C):
        cols = slice(h * DH_C, (h + 1) * DH_C)
        kh = k_ref[:, cols].astype(BF16)
        vh = v_ref[:, cols].astype(BF16)
        sc = lax.dot_general(q[:, cols].astype(BF16), kh, (((1,), (1,)), ((), ())),
                             preferred_element_type=F32) * (DH_C ** -0.5)
        e = jnp.exp(sc - jnp.max(sc, axis=-1, keepdims=True))
        p = e * (1.0 / jnp.sum(e, axis=-1, keepdims=True))
        o_ref[:, cols] = jnp.dot(p.astype(BF16), vh, preferred_element_type=F32)


def attention(x, w_in_bf, b_in, l, k, v, kv_index, tq):
    s, t, _ = x.shape
    lk, kc, vc = kv_index
    return pl.pallas_call(
        _attention_kernel,
        grid=(s, t // tq),
        in_specs=[pl.BlockSpec((None, tq, D_MODEL), lambda i, j: (i, j, 0)),
                  pl.BlockSpec((None, D_MODEL, D_C), lambda i, j: (l, 0, Q_BLOCK)),
                  pl.BlockSpec((None, 1, D_C), lambda i, j: (l, 0, Q_BLOCK)),
                  pl.BlockSpec((None, None, N_MEM, D_C), lambda i, j: (lk, i, 0, kc)),
                  pl.BlockSpec((None, None, N_MEM, D_C), lambda i, j: (lk, i, 0, vc))],
        out_specs=pl.BlockSpec((None, tq, D_C), lambda i, j: (i, j, 0)),
        out_shape=jax.ShapeDtypeStruct((s, t, D_C), F32),
        compiler_params=_params("arbitrary", "arbitrary"),
        name="attention",
    )(x, w_in_bf, b_in, k, v)


def _attention_rows_kernel(x_ref, wq_ref, bq_ref, k_ref, v_ref, o_ref, *, sb, t):
    n_rows = H_C * t
    n_cols = N_MEM * H_C
    row_head = lax.broadcasted_iota(jnp.int32, (n_rows, n_cols), 0) // t
    col_head = lax.broadcasted_iota(jnp.int32, (n_rows, n_cols), 1) % H_C
    own = row_head == col_head
    qs = (jnp.dot(x_ref[...].reshape(sb * t, D_MODEL).astype(BF16), wq_ref[...],
                  preferred_element_type=F32) + bq_ref[...])
    for s in range(sb):
        q = qs[s * t:(s + 1) * t, :]
        qa = jnp.concatenate([q[:, h * DH_C:(h + 1) * DH_C] for h in range(H_C)], axis=0)
        sc = lax.dot_general(qa.astype(BF16), k_ref[s].astype(BF16), (((1,), (1,)), ((), ())),
                             preferred_element_type=F32) * (DH_C ** -0.5)
        sc = jnp.where(own, sc, -jnp.inf)
        e = jnp.exp(sc - jnp.max(sc, axis=-1, keepdims=True))
        p = e * (1.0 / jnp.sum(e, axis=-1, keepdims=True))
        oa = jnp.dot(p.astype(BF16), v_ref[s].astype(BF16), preferred_element_type=F32)
        o_ref[s] = jnp.concatenate([oa[h * t:(h + 1) * t, :] for h in range(H_C)], axis=1)


def attention_rows(x, w_in_bf, b_in, l, k, v, sb):
    s, t, _ = x.shape
    rows = N_MEM * H_C
    return pl.pallas_call(
        functools.partial(_attention_rows_kernel, sb=sb, t=t),
        grid=(s // sb,),
        in_specs=[pl.BlockSpec((sb, t, D_MODEL), lambda i: (i, 0, 0)),
                  pl.BlockSpec((None, D_MODEL, D_C), lambda i: (l, 0, Q_BLOCK)),
                  pl.BlockSpec((None, 1, D_C), lambda i: (l, 0, Q_BLOCK)),
                  pl.BlockSpec((None, sb, rows, DH_C), lambda i: (l, i, 0, 0)),
                  pl.BlockSpec((None, sb, rows, DH_C), lambda i: (l, i, 0, 0))],
        out_specs=pl.BlockSpec((sb, t, D_C), lambda i: (i, 0, 0)),
        out_shape=jax.ShapeDtypeStruct((s, t, D_C), F32),
        compiler_params=_params("arbitrary"),
        name="attention_rows",
    )(x, w_in_bf, b_in, k, v)


def _mix_kernel(o_ref, x_ref, sa_ref, sb_ref, h0_ref, bin_ref,
                wca_ref, bca_ref, lag_ref, lab_ref, wao_ref,
                wcb_ref, bcb_ref, wrg_ref, bra_ref, brx_ref, lam_ref, wbo_ref,
                wco_ref, wmo_ref, l1g_ref, l1b_ref, win_hbm,
                x1_ref, nsa_ref, nsb_ref, hl_ref,
                ua_buf, xb_buf, a_buf, u_buf, hs_buf, h_carry, o_buf, x_buf, win_buf, win_sem,
                *, l, tt, nb, nt, x_bm):
    r = tt * nb
    ha = (K_A - 1) * nb
    hb = (K_B - 1) * nb
    i = pl.program_id(1)

    @pl.when((pl.program_id(0) == 0) & (i == 0))
    def _():
        load = pltpu.make_async_copy(win_hbm.at[l], win_buf, win_sem)
        load.start()
        load.wait()

    @pl.when(i == 0)
    def _():
        ua_buf[0:ha, :] = sa_ref[...].reshape(ha, D_A)
        xb_buf[0:hb, :] = sb_ref[...].reshape(hb, D_B)
        h_carry[...] = h0_ref[...]

    if x_bm:
        x = _rows_from_seq_major(x_ref, x_buf, nb)
    else:
        x = x_ref[...].reshape(r, D_MODEL)
    xb = x.astype(BF16)

    def zcols(lo, hi):
        return (jnp.dot(xb, win_buf[:, lo:hi], preferred_element_type=F32) + bin_ref[:, lo:hi])

    za = zcols(0, 2 * D_A)
    ua_buf[ha:ha + r, :] = za[:, :D_A] * jax.nn.sigmoid(za[:, D_A:])
    later_cols = [(2 * D_A + n * D_B, 2 * D_A + (n + 1) * D_B) for n in range(2)]
    later_cols += [(ZM_COL + n * D_MODEL, ZM_COL + (n + 1) * D_MODEL) for n in range(N_BRANCH)]
    taps_per_group = -(-K_A // len(later_cols))
    z_later = []
    c_a = jnp.broadcast_to(bca_ref[...], (r, D_A))
    for n, (lo, hi) in enumerate(later_cols):
        for k in range(n * taps_per_group, min((n + 1) * taps_per_group, K_A)):
            c_a = c_a + wca_ref[k:k + 1, :] * ua_buf[k * nb:k * nb + r, :]
        z_later.append(zcols(lo, hi))
    z_b, z_g, z_m0, z_m1, z_m2 = z_later
    y_a = _bdot(jax.nn.silu(_layer_norm(c_a, lag_ref[...], lab_ref[...])), wao_ref[...])
    nsa_ref[...] = ua_buf[r:r + ha, :].reshape(K_A - 1, nb, D_A)

    xb_buf[hb:hb + r, :] = z_b
    c_b = jnp.broadcast_to(bcb_ref[...], (r, D_B))
    for k in range(K_B):
        c_b = c_b + wcb_ref[k:k + 1, :] * xb_buf[k * nb:k * nb + r, :]
    nsb_ref[...] = xb_buf[r:r + hb, :].reshape(K_B - 1, nb, D_B)
    c_bf = c_b.astype(BF16)
    ra, rx = [], []
    for h in range(H_B):
        gh = jnp.dot(c_bf[:, h * BW_B:(h + 1) * BW_B], wrg_ref[h], preferred_element_type=F32)
        ra.append(gh[:, :BW_B])
        rx.append(gh[:, BW_B:])
    gate_r = jax.nn.sigmoid(jnp.concatenate(ra, axis=1) + bra_ref[...])
    gate_i = jax.nn.sigmoid(jnp.concatenate(rx, axis=1) + brx_ref[...])
    log_a = (-LRU_C) * gate_r * jax.nn.softplus(-lam_ref[...])
    a_buf[...] = jnp.exp(log_a)
    th = jnp.tanh(log_a)
    u_buf[...] = jnp.sqrt(-2.0 * th / (1.0 - th)) * (gate_i * c_b)
    h = h_carry[...]
    for t in range(tt):
        rows = slice(t * nb, (t + 1) * nb)
        h = a_buf[rows, :] * h + u_buf[rows, :]
        hs_buf[rows, :] = h
    h_carry[...] = h
    hl_ref[...] = h
    y_b = _bdot(hs_buf[...] * jax.nn.gelu(z_g), wbo_ref[...])

    if nt > 1:
        ua_buf[0:ha, :] = ua_buf[r:r + ha, :]
        xb_buf[0:hb, :] = xb_buf[r:r + hb, :]

    y_c = _bdot(_rows_from_seq_major(o_ref, o_buf, nb), wco_ref[...])

    merged = jax.nn.sigmoid(z_m0) * y_a
    merged = merged + jax.nn.sigmoid(z_m1) * y_b
    merged = merged + jax.nn.sigmoid(z_m2) * y_c
    mix = _bdot(merged, wmo_ref[...])
    x1_ref[...] = _layer_norm(ALPHA * x + mix, l1g_ref[...], l1b_ref[...]).reshape(tt, nb, D_MODEL)


def mix(l, o, x, sa_tm, sb_tm, h0, w, w_in_bf, tt, nb, x_bm):
    nbt, t = o.shape[:2]
    nt = t // tt
    assert nt == 1 or tt >= K_A - 1
    r = tt * nb
    grid = (nbt // nb, nt)

    def bm_spec(width):
        return pl.BlockSpec((nb, tt, width), lambda g, i: (g, i, 0))

    def zspec(width, blk):
        return pl.BlockSpec((tt, nb, width), lambda g, i: (i, g, blk))

    def full(a):
        nd = a.ndim
        return pl.BlockSpec(a.shape, lambda g, i: (0,) * nd)

    weights = [w["b_in"], w["w_conv_a"], w["b_conv_a"], w["ln_a_g"], w["ln_a_b"], w["w_a_out"],
               w["w_conv_b"], w["b_conv_b"], w["w_rg"], w["b_rg_a"], w["b_rg_x"], w["lam"],
               w["w_b_out"], w["w_c_out"], w["w_mix_out"], w["ln1_g"], w["ln1_b"]]
    in_specs = ([bm_spec(D_C), bm_spec(D_MODEL) if x_bm else zspec(D_MODEL, 0),
                 pl.BlockSpec((K_A - 1, nb, D_A), lambda g, i: (0, g, 0)),
                 pl.BlockSpec((K_B - 1, nb, D_B), lambda g, i: (0, g, 0)),
                 pl.BlockSpec((nb, D_B), lambda g, i: (g, 0))]
                + [full(a) for a in weights]
                + [pl.BlockSpec(memory_space=pl.ANY)])
    out_specs = [zspec(D_MODEL, 0),
                 pl.BlockSpec((K_A - 1, nb, D_A), lambda g, i: (0, g, 0)),
                 pl.BlockSpec((K_B - 1, nb, D_B), lambda g, i: (0, g, 0)),
                 pl.BlockSpec((nb, D_B), lambda g, i: (g, 0))]
    out_shape = [jax.ShapeDtypeStruct((t, nbt, D_MODEL), F32),
                 jax.ShapeDtypeStruct((K_A - 1, nbt, D_A), F32),
                 jax.ShapeDtypeStruct((K_B - 1, nbt, D_B), F32),
                 jax.ShapeDtypeStruct((nbt, D_B), F32)]
    scratch = [pltpu.VMEM(((K_A - 1) * nb + r, D_A), F32),
               pltpu.VMEM(((K_B - 1) * nb + r, D_B), F32),
               pltpu.VMEM((r, D_B), F32), pltpu.VMEM((r, D_B), F32), pltpu.VMEM((r, D_B), F32),
               pltpu.VMEM((nb, D_B), F32), pltpu.VMEM((D_C // LANES, r, LANES), F32),
               pltpu.VMEM((D_MODEL // LANES, r, LANES), F32),
               pltpu.VMEM((D_MODEL, D_IN), BF16), pltpu.SemaphoreType.DMA(())]
    return pl.pallas_call(
        functools.partial(_mix_kernel, l=l, tt=tt, nb=nb, nt=nt, x_bm=x_bm),
        grid=grid, in_specs=in_specs, out_specs=out_specs, out_shape=out_shape,
        scratch_shapes=scratch,
        compiler_params=_params("arbitrary", "arbitrary"),
        name="mix",
    )(o, x, sa_tm, sb_tm, h0, *weights, w_in_bf)


def _ffn_kernel(x_ref, wg_ref, wu_ref, wd_ref, g_ref, b_ref, o_ref, obm_ref, xb_ref, acc_ref,
                stage_ref, *, nb_seq):
    j = pl.program_id(1)

    @pl.when(j == 0)
    def _():
        xb_ref[...] = x_ref[...].astype(BF16)
        acc_ref[...] = jnp.zeros_like(acc_ref)

    xb = xb_ref[...]
    hg = jnp.dot(xb, wg_ref[...].astype(BF16), preferred_element_type=F32)
    hu = jnp.dot(xb, wu_ref[...].astype(BF16), preferred_element_type=F32)
    acc_ref[...] += _bdot(jax.nn.silu(hg) * hu, wd_ref[...])

    @pl.when(j == pl.num_programs(1) - 1)
    def _():
        res = _layer_norm(ALPHA * x_ref[...] + acc_ref[...], g_ref[...], b_ref[...])
        o_ref[...] = res
        _seq_major_from_rows(obm_ref, stage_ref, res, nb_seq)


def ffn_dense(x, wg, wu, wd, g, b, l, nb_seq):
    m = x.shape[0]
    return pl.pallas_call(
        functools.partial(_ffn_kernel, nb_seq=nb_seq),
        grid=(m // TM, D_FF // TF),
        in_specs=[pl.BlockSpec((TM, D_MODEL), lambda i, j: (i, 0)),
                  pl.BlockSpec((None, D_MODEL, TF), lambda i, j: (l, 0, j)),
                  pl.BlockSpec((None, D_MODEL, TF), lambda i, j: (l, 0, j)),
                  pl.BlockSpec((None, TF, D_MODEL), lambda i, j: (l, j, 0)),
                  pl.BlockSpec((1, D_MODEL), lambda i, j: (0, 0)),
                  pl.BlockSpec((1, D_MODEL), lambda i, j: (0, 0))],
        out_specs=[pl.BlockSpec((TM, D_MODEL), lambda i, j: (i, 0)),
                   pl.BlockSpec((nb_seq, TM // nb_seq, D_MODEL), lambda i, j: (0, i, 0))],
        out_shape=[jax.ShapeDtypeStruct((m, D_MODEL), F32),
                   jax.ShapeDtypeStruct((nb_seq, m // nb_seq, D_MODEL), F32)],
        scratch_shapes=[pltpu.VMEM((TM, D_MODEL), BF16), pltpu.VMEM((TM, D_MODEL), F32),
                        pltpu.VMEM((TOK, TM, LANES), F32)],
        compiler_params=_params("arbitrary", "arbitrary"),
        name="ffn_dense",
    )(x, wg, wu, wd, g.reshape(1, -1), b.reshape(1, -1))


def _router_kernel(x_ref, w_ref, b_ref, sel_ref, prob_ref):
    x = x_ref[...]
    w = w_ref[...]
    xh = x.astype(BF16)
    wh = w.astype(BF16)
    xl = (x - xh.astype(F32)).astype(BF16)
    wl = (w - wh.astype(F32)).astype(BF16)
    dot = functools.partial(jnp.dot, preferred_element_type=F32)
    logits = dot(xh, wh) + (dot(xh, wl) + dot(xl, wh) + dot(xl, wl)) + b_ref[...]
    lane = lax.broadcasted_iota(jnp.int32, logits.shape, 1)
    neg = jnp.float32(-jnp.inf)
    l1 = jnp.where(lane < N_EXPERTS, logits, neg)
    m1 = jnp.max(l1, axis=1, keepdims=True)
    i1 = jnp.min(jnp.where(l1 == m1, lane, LANES), axis=1, keepdims=True)
    l2 = jnp.where(lane == i1, neg, l1)
    m2 = jnp.max(l2, axis=1, keepdims=True)
    i2 = jnp.min(jnp.where(l2 == m2, lane, LANES), axis=1, keepdims=True)
    e2 = jnp.exp(m2 - m1)
    den = 1.0 + e2
    sel_ref[...] = jnp.where(lane == 0, i1, jnp.where(lane == 1, i2, 0))
    prob_ref[...] = jnp.where(lane == 0, 1.0 / den, jnp.where(lane == 1, e2 / den, 0.0))


def router(x, w_router, b_router):
    m = x.shape[0]
    wp = jnp.zeros((D_MODEL, LANES), F32).at[:, :N_EXPERTS].set(w_router)
    bp = jnp.zeros((1, LANES), F32).at[0, :N_EXPERTS].set(b_router)
    return pl.pallas_call(
        _router_kernel,
        grid=(m // TM,),
        in_specs=[pl.BlockSpec((TM, D_MODEL), lambda i: (i, 0)),
                  pl.BlockSpec((D_MODEL, LANES), lambda i: (0, 0)),
                  pl.BlockSpec((1, LANES), lambda i: (0, 0))],
        out_specs=[pl.BlockSpec((TM, LANES), lambda i: (i, 0)),
                   pl.BlockSpec((TM, LANES), lambda i: (i, 0))],
        out_shape=[jax.ShapeDtypeStruct((m, LANES), jnp.int32),
                   jax.ShapeDtypeStruct((m, LANES), F32)],
        compiler_params=_params("arbitrary"),
        name="router",
    )(x, wp, bp)


def _route(sel, tm, n_tiles):
    e = sel[:, :TOP_K].reshape(-1)
    onehot = (e[:, None] == jnp.arange(N_EXPERTS, dtype=jnp.int32)[None, :]).astype(jnp.int32)
    csum = jnp.cumsum(onehot, axis=0)
    rank = jnp.sum(onehot * csum, axis=1) - 1
    counts = csum[-1]
    padded = ((counts + tm - 1) // tm) * tm
    ends = jnp.cumsum(padded)
    starts = ends - padded
    dest = jnp.sum(onehot * starts[None, :], axis=1) + rank
    first_row = jnp.arange(n_tiles, dtype=jnp.int32) * tm
    tile_expert = jnp.sum((first_row[:, None] >= ends[None, :]).astype(jnp.int32), axis=1)
    tile_expert = jnp.minimum(tile_expert, N_EXPERTS - 1)
    n_used = (ends[-1] // tm).reshape(1)
    return dest.astype(jnp.int32), tile_expert.astype(jnp.int32), n_used.astype(jnp.int32)


def _token_copy(src_ref, dst_ref, sem, s_tok, d_tok, n_tok):
    s0 = pl.multiple_of(s_tok * TOK, TOK)
    d0 = pl.multiple_of(d_tok * TOK, TOK)
    return pltpu.make_async_copy(src_ref.at[pl.ds(s0, n_tok * TOK), :],
                                 dst_ref.at[pl.ds(d0, n_tok * TOK), :], sem)


def _to_token_tiles(dst_ref, src, rows):
    for s in range(TOK):
        dst_ref[pl.ds(s, rows, stride=TOK), :] = src[:, s * LANES:(s + 1) * LANES]


def _dispatch_kernel(didx_ref, x_ref, init_ref, out_ref, tok_buf, sem):
    del init_ref
    _to_token_tiles(tok_buf, x_ref[...], TM)

    def issue(t, carry):
        for k in range(TOP_K):
            _token_copy(tok_buf, out_ref, sem, t, didx_ref[0, TOP_K * t + k], 1).start(
                priority=k % N_DMA_PRIORITIES)
        return carry

    lax.fori_loop(0, TM, issue, 0)
    for _ in range(TOP_K):
        _token_copy(tok_buf, out_ref, sem, 0, 0, TM).wait()


def dispatch(x, dest, grouped):
    m = x.shape[0]
    nc = m // TM
    return pl.pallas_call(
        _dispatch_kernel,
        grid=(nc,),
        in_specs=[pl.BlockSpec((None, 1, TOP_K * TM), lambda c: (c, 0, 0),
                               memory_space=pltpu.SMEM),
                  pl.BlockSpec((TM, D_MODEL), lambda c: (c, 0)),
                  pl.BlockSpec(memory_space=pl.ANY)],
        out_specs=pl.BlockSpec(memory_space=pl.ANY),
        out_shape=jax.ShapeDtypeStruct(grouped.shape, grouped.dtype),
        scratch_shapes=[pltpu.VMEM((TM * TOK, LANES), F32), pltpu.SemaphoreType.DMA(())],
        input_output_aliases={2: 0},
        compiler_params=_params("arbitrary"),
        name="dispatch",
    )(dest.reshape(nc, 1, TOP_K * TM), x, grouped)


def _gmm_kernel(te_ref, nu_ref, x_ref, wg_ref, wu_ref, wd_ref, y_ref, xb_ref, acc_ref):
    del te_ref
    i = pl.program_id(0)
    j = pl.program_id(1)

    @pl.when((i >= nu_ref[0]) & (j == 0))
    def _():
        y_ref[...] = jnp.zeros_like(y_ref)

    @pl.when(i < nu_ref[0])
    def _():
        @pl.when(j == 0)
        def _():
            for s in range(TOK):
                xb_ref[:, s * LANES:(s + 1) * LANES] = (
                    x_ref[pl.ds(s, TM, stride=TOK), :].astype(BF16))
            acc_ref[...] = jnp.zeros_like(acc_ref)

        xb = xb_ref[...]
        hg = jnp.dot(xb, wg_ref[...].astype(BF16), preferred_element_type=F32)
        hu = jnp.dot(xb, wu_ref[...].astype(BF16), preferred_element_type=F32)
        acc_ref[...] += _bdot(jax.nn.silu(hg) * hu, wd_ref[...])

        @pl.when(j == pl.num_programs(1) - 1)
        def _():
            _to_token_tiles(y_ref, acc_ref, TM)


def gmm(xs, tile_expert, n_used, wg, wu, wd, l):
    rows = xs.shape[0] // TOK
    n_tiles = rows // TM
    nf = D_FF // TF

    def tile(i, nu):
        return jnp.minimum(i, nu[0] - 1)

    def ff(i, j, nu):
        return jnp.where(i < nu[0], j, nf - 1)

    grid_spec = pltpu.PrefetchScalarGridSpec(
        num_scalar_prefetch=2,
        grid=(n_tiles, nf),
        in_specs=[pl.BlockSpec((TM * TOK, LANES), lambda i, j, te, nu: (i, 0)),
                  pl.BlockSpec((None, None, D_MODEL, TF),
                               lambda i, j, te, nu: (l, te[tile(i, nu)], 0, ff(i, j, nu))),
                  pl.BlockSpec((None, None, D_MODEL, TF),
                               lambda i, j, te, nu: (l, te[tile(i, nu)], 0, ff(i, j, nu))),
                  pl.BlockSpec((None, None, TF, D_MODEL),
                               lambda i, j, te, nu: (l, te[tile(i, nu)], ff(i, j, nu), 0))],
        out_specs=pl.BlockSpec((TM * TOK, LANES), lambda i, j, te, nu: (i, 0)),
        scratch_shapes=[pltpu.VMEM((TM, D_MODEL), BF16), pltpu.VMEM((TM, D_MODEL), F32)],
    )
    return pl.pallas_call(
        _gmm_kernel,
        grid_spec=grid_spec,
        out_shape=jax.ShapeDtypeStruct(xs.shape, F32),
        compiler_params=_params("arbitrary", "arbitrary"),
        name="gmm",
    )(tile_expert, n_used, xs, wg, wu, wd)


def _combine_kernel(didx_ref, x_ref, prob_ref, g_ref, b_ref, y_ref, o_ref, ybuf, stage_ref, sem,
                    *, nb_seq):
    def issue(t, carry):
        for k in range(TOP_K):
            _token_copy(y_ref, ybuf, sem, didx_ref[0, TOP_K * t + k], k * TM + t, 1).start(
                priority=k % N_DMA_PRIORITIES)
        return carry

    lax.fori_loop(0, TM, issue, 0, unroll=ISSUE_UNROLL)
    for k in range(TOP_K):
        _token_copy(y_ref, ybuf, sem, 0, k * TM, TM).wait()
    p0 = prob_ref[:, 0:1]
    p1 = prob_ref[:, 1:2]
    f = jnp.concatenate(
        [p0 * ybuf[pl.ds(s, TM, stride=TOK), :] + p1 * ybuf[pl.ds(TM * TOK + s, TM, stride=TOK), :]
         for s in range(TOK)], axis=1)
    res = _layer_norm(ALPHA * x_ref[...] + f, g_ref[...], b_ref[...])
    _seq_major_from_rows(o_ref, stage_ref, res, nb_seq)


def combine(x, y, dest, prob, g, b, nb_seq):
    m = x.shape[0]
    nc = m // TM
    return pl.pallas_call(
        functools.partial(_combine_kernel, nb_seq=nb_seq),
        grid=(nc,),
        in_specs=[pl.BlockSpec((None, 1, TOP_K * TM), lambda i: (i, 0, 0),
                               memory_space=pltpu.SMEM),
                  pl.BlockSpec((TM, D_MODEL), lambda i: (i, 0)),
                  pl.BlockSpec((TM, LANES), lambda i: (i, 0)),
                  pl.BlockSpec((1, D_MODEL), lambda i: (0, 0)),
                  pl.BlockSpec((1, D_MODEL), lambda i: (0, 0)),
                  pl.BlockSpec(memory_space=pl.ANY)],
        out_specs=pl.BlockSpec((nb_seq, TM // nb_seq, D_MODEL), lambda i: (0, i, 0)),
        out_shape=jax.ShapeDtypeStruct((nb_seq, m // nb_seq, D_MODEL), F32),
        scratch_shapes=[pltpu.VMEM((TOP_K * TM * TOK, LANES), F32),
                        pltpu.VMEM((TOK, TM, LANES), F32), pltpu.SemaphoreType.DMA(())],
        compiler_params=_params("arbitrary"),
        name="combine",
    )(dest.reshape(nc, 1, TOP_K * TM), x, prob, g.reshape(1, -1), b.reshape(1, -1), y)


def moe_routed(xs_list, nb_seqs, p, l):
    j = l // 2
    routed = [router(x, p["w_router"][j], p["b_router"][j]) for x in xs_list]
    sel = jnp.concatenate([r[0] for r in routed], axis=0)
    n_pairs = TOP_K * sel.shape[0]
    n_tiles = (n_pairs + N_EXPERTS * (TM - 1)) // TM
    dest, tile_expert, n_used = _route(sel, TM, n_tiles)
    bounds = [0]
    for x in xs_list:
        bounds.append(bounds[-1] + TOP_K * x.shape[0])

    grouped = jnp.zeros((n_tiles * TM * TOK, LANES), F32)
    for x, lo, hi in zip(xs_list, bounds[:-1], bounds[1:]):
        grouped = dispatch(x, dest[lo:hi], grouped)
    y = gmm(grouped, tile_expert, n_used, p["w_e_gate"], p["w_e_up"], p["w_e_down"], j)
    return [combine(x, y, dest[lo:hi], prob, p["ln2_g"][l], p["ln2_b"][l], nb)
            for x, nb, (_, prob), lo, hi in zip(xs_list, nb_seqs, routed, bounds[:-1], bounds[1:])]


def _layer_weights(l, p):
    row = lambda a: a[l].reshape(1, -1)
    return {
        "b_in": row(p["b_in"]),
        "w_conv_a": p["w_conv_a"][l], "b_conv_a": row(p["b_conv_a"]),
        "ln_a_g": row(p["ln_a_g"]), "ln_a_b": row(p["ln_a_b"]),
        "w_a_out": p["w_a_out"][l].astype(BF16),
        "w_conv_b": p["w_conv_b"][l], "b_conv_b": row(p["b_conv_b"]),
        "w_rg": jnp.concatenate([p["w_rg_a"][l], p["w_rg_x"][l]], axis=-1).astype(BF16),
        "b_rg_a": row(p["b_rg_a"]), "b_rg_x": row(p["b_rg_x"]), "lam": row(p["lru_lambda"]),
        "w_b_out": p["w_b_out"][l].astype(BF16), "w_c_out": p["w_c_out"][l].astype(BF16),
        "w_mix_out": p["w_mix_out"][l].astype(BF16),
        "ln1_g": row(p["ln1_g"]), "ln1_b": row(p["ln1_b"]),
    }


def _mix_layer(l, w, w_in_bf, x, o, sa_tm, sb_tm, h0, tt, nb, x_bm):
    nbt, t = o.shape[:2]
    x1, nsa, nsb, hl = mix(l, o, x, sa_tm, sb_tm, h0, w, w_in_bf, tt, nb, x_bm)
    return x1.reshape(t * nbt, D_MODEL), nsa, nsb, hl


def _to_tm(a):
    return jnp.transpose(a, (1, 0, 2))


def kernel(x_prompt, x_sample, state_conv_a, state_conv_b, state_rglru, cache_mem_k, cache_mem_v, mem_prompt, w_in, b_in, w_conv_a, b_conv_a, ln_a_g, ln_a_b, w_a_out, w_conv_b, b_conv_b, w_rg_a, b_rg_a, w_rg_x, b_rg_x, lru_lambda, w_b_out, w_mem_kv, w_c_out, w_mix_out, ln1_g, ln1_b, w_ff_gate, w_ff_up, w_ff_down, w_router, b_router, w_e_gate, w_e_up, w_e_down, ln2_g, ln2_b):
    p = dict(w_in=w_in, b_in=b_in, w_conv_a=w_conv_a, b_conv_a=b_conv_a, ln_a_g=ln_a_g,
             ln_a_b=ln_a_b, w_a_out=w_a_out, w_conv_b=w_conv_b, b_conv_b=b_conv_b,
             w_rg_a=w_rg_a, b_rg_a=b_rg_a, w_rg_x=w_rg_x, b_rg_x=b_rg_x, lru_lambda=lru_lambda,
             w_b_out=w_b_out, w_c_out=w_c_out, w_mix_out=w_mix_out, ln1_g=ln1_g, ln1_b=ln1_b,
             w_ff_gate=w_ff_gate, w_ff_up=w_ff_up, w_ff_down=w_ff_down, w_router=w_router,
             b_router=b_router, w_e_gate=w_e_gate, w_e_up=w_e_up, w_e_down=w_e_down,
             ln2_g=ln2_g, ln2_b=ln2_b)
    bp, tp, _ = x_prompt.shape
    bs, ts, _ = x_sample.shape
    n_mem = mem_prompt.shape[1]

    xp_bm, xs_bm = x_prompt, x_sample
    xp_tm = xs_tm = None
    w_in_bf = w_in.astype(BF16)
    b_in3 = b_in.reshape(DEPTH, 1, D_IN)
    zero_a = jnp.zeros((K_A - 1, bp, D_A), F32)
    zero_b = jnp.zeros((K_B - 1, bp, D_B), F32)
    zero_h = jnp.zeros((bp, D_B), F32)
    zero_bias = jnp.zeros((DEPTH, 1, 2 * D_C), F32)
    cache_k = cache_mem_k.reshape(DEPTH, bs, n_mem * H_C, DH_C)
    cache_v = cache_mem_v.reshape(DEPTH, bs, n_mem * H_C, DH_C)
    outs = {k: [] for k in ("pa", "pb", "ph", "pk", "pv", "sa", "sb", "sh")}
    for l in range(DEPTH):
        w = _layer_weights(l, p)
        kv = matmul_bias(mem_prompt.reshape(bp * n_mem, D_MODEL), w_mem_kv, zero_bias, l,
                         Z_BLOCK, "kv_proj").reshape(bp, n_mem, 2 * D_C)
        mk = kv[..., :D_C]
        mv = kv[..., D_C:]
        kv4 = kv.reshape(1, bp, n_mem, 2 * D_C)
        op = attention(xp_bm, w_in_bf, b_in3, l, kv4, kv4, (0, 0, 1), tq=1024)
        xp1, na, nb_, nh = _mix_layer(l, w, w_in_bf, xp_bm if xp_tm is None else xp_tm, op,
                                      zero_a, zero_b, zero_h, tt=64, nb=bp, x_bm=xp_tm is None)
        outs["pa"].append(_to_tm(na))
        outs["pb"].append(_to_tm(nb_))
        outs["ph"].append(nh)
        outs["pk"].append(mk.reshape(bp, n_mem, H_C, DH_C))
        outs["pv"].append(mv.reshape(bp, n_mem, H_C, DH_C))
        os_ = attention_rows(xs_bm, w_in_bf, b_in3, l, cache_k, cache_v, sb=8)
        xs1, na, nb_, nh = _mix_layer(l, w, w_in_bf, xs_bm if xs_tm is None else xs_tm, os_,
                                      _to_tm(state_conv_a[l]), _to_tm(state_conv_b[l]),
                                      state_rglru[l], tt=ts, nb=32, x_bm=xs_tm is None)
        outs["sa"].append(_to_tm(na))
        outs["sb"].append(_to_tm(nb_))
        outs["sh"].append(nh)
        if l % 2 == 0:
            j = l // 2
            (xp_rows, xp_bm), (xs_rows, xs_bm) = [
                ffn_dense(x, w_ff_gate, w_ff_up, w_ff_down, ln2_g[l], ln2_b[l], j, nb)
                for x, nb in ((xp1, bp), (xs1, bs))]
            xp_tm = xp_rows.reshape(tp, bp, D_MODEL)
            xs_tm = xs_rows.reshape(ts, bs, D_MODEL)
        else:
            xp_bm, xs_bm = moe_routed([xp1, xs1], [bp, bs], p, l)
            xp_tm = xs_tm = None

    st = lambda k: jnp.stack(outs[k])
    return (xp_bm, xs_bm, st("pa"), st("pb"), st("ph"), st("pk"), st("pv"),
            st("sa"), st("sb"), st("sh"))
```
